```python
import math, functools
import jax, jax.numpy as jnp
from jax import lax
import numpy as np

D_MODEL = 2048
BATCH = 2
SEQ = 4096
DEPTH = 4
DEC_BATCH = 8
DEC_SEQ = 4
PAST_LEN = 16384
PAGE_SIZE = 128

ATT_H = 8
ATT_HD = 64
ATT_VD = 2 * ATT_HD
ROT_DIM = ATT_HD // 4
ROPE_THETA = 500000.0
Q_BLOCK = 128
HG_H = 8
HG_K = 128
HG_V = 128
GLA_CHUNK = 64
D_FF = 4 * D_MODEL
EPS = 1e-6

ATT_QK_W = ATT_H * 2 * ATT_HD
ATT_V_W = ATT_H * ATT_VD
HG_K_W = HG_H * HG_K
HG_V_W = HG_H * HG_V
IN_SPLITS = (ATT_QK_W, ATT_QK_W, ATT_V_W, HG_K_W, HG_K_W, HG_V_W, HG_V_W, D_MODEL, D_MODEL)
IN_WIDTH = sum(IN_SPLITS)

kernel_name = 'hybrid_diffattn_hgrn2_step'


def _rmsnorm(x, g):
    xf = x.astype(jnp.float32)
    y = xf * lax.rsqrt(jnp.mean(xf * xf, axis=-1, keepdims=True) + EPS) * g.astype(jnp.float32)
    return y.astype(x.dtype)


def _rope(x, pos):
    inv = ROPE_THETA ** (-jnp.arange(0, ROT_DIM, 2, dtype=jnp.float32) / ROT_DIM)
    ang = pos.astype(jnp.float32)[:, None] * inv[None, :]
    cos = jnp.cos(ang)[None, :, None, :]
    sin = jnp.sin(ang)[None, :, None, :]
    xr = x[..., :ROT_DIM].astype(jnp.float32)
    x1, x2 = xr[..., :ROT_DIM // 2], xr[..., ROT_DIM // 2:]
    rot = jnp.concatenate([x1 * cos - x2 * sin, x2 * cos + x1 * sin], axis=-1).astype(x.dtype)
    return jnp.concatenate([rot, x[..., ROT_DIM:]], axis=-1)


def _lambda_init(l):
    return 0.8 - 0.6 * math.exp(-0.3 * l)


def _diff_attn_prompt(q1, q2, k1, k2, v, lam):
    B, T, H, _ = q1.shape
    nb = T // Q_BLOCK
    scale = ATT_HD ** -0.5
    kpos = jnp.arange(T)

    def blocks(a):
        return a.reshape(B, nb, Q_BLOCK, H, a.shape[-1]).transpose(1, 0, 2, 3, 4)

    def one(args):
        qb1, qb2, start = args
        qpos = start + jnp.arange(Q_BLOCK)
        mask = kpos[None, :] <= qpos[:, None]

        def probs(qb, kk):
            s = jnp.einsum('bqhd,bkhd->bhqk', qb, kk, preferred_element_type=jnp.float32) * scale
            return jax.nn.softmax(jnp.where(mask, s, -jnp.inf), axis=-1)

        p = probs(qb1, k1) - lam * probs(qb2, k2)
        return jnp.einsum('bhqk,bkhv->bqhv', p, v.astype(jnp.float32))

    out = lax.map(one, (blocks(q1), blocks(q2), jnp.arange(nb) * Q_BLOCK))
    return out.transpose(1, 0, 2, 3, 4).reshape(B, T, H, v.shape[-1])


def _diff_attn_sample(q1, q2, k1, k2, v, lam, k_past, v_past):
    T = q1.shape[1]
    P = k_past.shape[1]
    scale = ATT_HD ** -0.5
    causal = jnp.tril(jnp.ones((T, T), dtype=bool))

    def probs(q, kp, kn):
        sp = jnp.einsum('bqhd,bkhd->bhqk', q, kp, preferred_element_type=jnp.float32) * scale
        sn = jnp.einsum('bqhd,bkhd->bhqk', q, kn, preferred_element_type=jnp.float32) * scale
        sn = jnp.where(causal, sn, -jnp.inf)
        return jax.nn.softmax(jnp.concatenate([sp, sn], axis=-1), axis=-1)

    p = probs(q1, k_past[..., :ATT_HD], k1) - lam * probs(q2, k_past[..., ATT_HD:], k2)
    return (jnp.einsum('bhqk,bkhv->bqhv', p[..., :P], v_past.astype(jnp.float32))
            + jnp.einsum('bhqk,bkhv->bqhv', p[..., P:], v.astype(jnp.float32)))


def _gla_chunked(q, k, v, logg, S0):
    B, T, H, K = q.shape
    C = GLA_CHUNK if T % GLA_CHUNK == 0 else T
    n = T // C
    causal = jnp.tril(jnp.ones((C, C), dtype=bool))

    def to_chunks(a):
        return a.reshape(B, n, C, H, a.shape[-1]).transpose(1, 0, 2, 3, 4)

    def step(S, inp):
        qc, kc, vc, gc = inp
        b = jnp.cumsum(gc, axis=1)
        o = jnp.einsum('bthk,bhkv->bthv', qc * jnp.exp(b), S)
        diff = b[:, :, None] - b[:, None, :]
        decay = jnp.exp(jnp.where(causal[None, :, :, None, None], diff, -jnp.inf))
        A = jnp.einsum('bthk,btshk,bshk->bhts', qc, decay, kc)
        o = o + jnp.einsum('bhts,bshv->bthv', A, vc)
        bC = b[:, -1]
        S = jnp.exp(bC)[..., None] * S + jnp.einsum('bshk,bshv->bhkv', kc * jnp.exp(bC[:, None] - b), vc)
        return S, o

    S, o = lax.scan(step, S0, (to_chunks(q), to_chunks(k), to_chunks(v), to_chunks(logg)))
    o = o.transpose(1, 0, 2, 3, 4).reshape(B, T, H, v.shape[-1])
    return o, S


def _layer(x, pos, l, S0, attend, p):
    f32 = jnp.float32
    B, T, _ = x.shape
    h = _rmsnorm(x, p['g_pre_mix'])
    proj = h @ p['w_in']
    offs = np.cumsum(IN_SPLITS)[:-1].tolist()
    q, k, v, hq, hf, hi, hg, ga, gb = jnp.split(proj, offs, axis=-1)
    q = q.reshape(B, T, ATT_H, 2, ATT_HD)
    k = k.reshape(B, T, ATT_H, 2, ATT_HD)
    q1, q2 = _rope(q[:, :, :, 0], pos), _rope(q[:, :, :, 1], pos)
    k1, k2 = _rope(k[:, :, :, 0], pos), _rope(k[:, :, :, 1], pos)
    v = v.reshape(B, T, ATT_H, ATT_VD)
    lam_init = _lambda_init(l)
    lam = (jnp.exp(jnp.sum(p['lambda_q1'].astype(f32) * p['lambda_k1'].astype(f32)))
           - jnp.exp(jnp.sum(p['lambda_q2'].astype(f32) * p['lambda_k2'].astype(f32))) + lam_init)
    att = attend(q1, q2, k1, k2, v, lam)
    att = (_rmsnorm(att, p['subln_g']) * (1.0 - lam_init)).astype(x.dtype).reshape(B, T, ATT_V_W)
    hq = jax.nn.silu(hq.astype(f32)).reshape(B, T, HG_H, HG_K) * (HG_K ** -0.5)
    fpre = hf.astype(f32).reshape(B, T, HG_H, HG_K)
    if l == 0:
        gval = jax.nn.sigmoid(fpre)
        logg = jax.nn.log_sigmoid(fpre)
    else:
        lb = p['lb'].reshape(HG_H, HG_K)
        gval = lb + (1.0 - lb) * jax.nn.sigmoid(fpre)
        logg = jnp.log(gval)
    hk = 1.0 - gval
    hv = hi.astype(f32).reshape(B, T, HG_H, HG_V)
    o, S = _gla_chunked(hq, hk, hv, logg, S0.astype(f32))
    o = _rmsnorm(o, p['hg_norm_g']) * jax.nn.silu(hg.astype(f32)).reshape(B, T, HG_H, HG_V)
    o = o.astype(x.dtype).reshape(B, T, HG_V_W)
    mixed = (jax.nn.sigmoid(ga.astype(f32)) * (att @ p['w_att_out']).astype(f32)
             + jax.nn.sigmoid(gb.astype(f32)) * (o @ p['w_hg_out']).astype(f32))
    y = mixed.astype(x.dtype) @ p['w_out']
    x = x + _rmsnorm(y, p['g_post_mix'])
    h = _rmsnorm(x, p['g_pre_ffn'])
    u = jnp.square(jax.nn.relu(h @ p['w_up']))
    x = x + _rmsnorm(u @ p['w_down'], p['g_post_ffn'])
    k_rows = jnp.concatenate([k1, k2], axis=-1)
    return x, k_rows, v, S


def setup_inputs(seed: int = 0) -> dict:
    key = jax.random.key(seed)
    ks = jax.random.split(key, 24)
    f32 = jnp.float32
    n_pages = PAST_LEN // PAGE_SIZE
    n_used = DEC_BATCH * n_pages
    n_pool = (n_used * 5 + 3) // 4

    def nrm(k, shape, scale):
        return jax.random.normal(k, shape, f32) * scale

    def gain(k, shape):
        return 1.0 + 0.05 * jax.random.normal(k, shape, f32)

    page_table = jax.random.permutation(ks[0], n_pool)[:n_used].reshape(DEC_BATCH, n_pages).astype(jnp.int32)
    return {
        'x_prompt': nrm(ks[1], (BATCH, SEQ, D_MODEL), 1.0),
        'x_sample': nrm(ks[2], (DEC_BATCH, DEC_SEQ, D_MODEL), 1.0),
        'cache_k': nrm(ks[3], (DEPTH, n_pool, PAGE_SIZE, ATT_H, 2 * ATT_HD), 1.0),
        'cache_v': nrm(ks[4], (DEPTH, n_pool, PAGE_SIZE, ATT_H, ATT_VD), 1.0),
        'state_hgrn': nrm(ks[5], (DEPTH, DEC_BATCH, HG_H, HG_K, HG_V), 0.5),
        'page_table': page_table,
        'g_pre_mix': gain(ks[6], (DEPTH, D_MODEL)),
        'w_in': nrm(ks[7], (DEPTH, D_MODEL, IN_WIDTH), D_MODEL ** -0.5),
        'lambda_q1': nrm(ks[8], (DEPTH, ATT_HD), 0.1),
        'lambda_k1': nrm(ks[9], (DEPTH, ATT_HD), 0.1),
        'lambda_q2': nrm(ks[10], (DEPTH, ATT_HD), 0.1),
        'lambda_k2': nrm(ks[11], (DEPTH, ATT_HD), 0.1),
        'subln_g': gain(ks[12], (DEPTH, ATT_VD)),
        'hg_lower_bounds': nrm(ks[13], (DEPTH, HG_K_W), 0.1),
        'hg_norm_g': gain(ks[14], (DEPTH, HG_V)),
        'w_att_out': nrm(ks[15], (DEPTH, ATT_V_W, D_MODEL), ATT_V_W ** -0.5),
        'w_hg_out': nrm(ks[16], (DEPTH, HG_V_W, D_MODEL), HG_V_W ** -0.5),
        'w_out': nrm(ks[17], (DEPTH, D_MODEL, D_MODEL), D_MODEL ** -0.5),
        'g_post_mix': gain(ks[18], (DEPTH, D_MODEL)),
        'g_pre_ffn': gain(ks[19], (DEPTH, D_MODEL)),
        'w_up': nrm(ks[20], (DEPTH, D_MODEL, D_FF), D_MODEL ** -0.5),
        'w_down': nrm(ks[21], (DEPTH, D_FF, D_MODEL), D_FF ** -0.5),
        'g_post_ffn': gain(ks[22], (DEPTH, D_MODEL)),
    }


def reference(x_prompt, x_sample, cache_k, cache_v, state_hgrn, page_table,
              g_pre_mix, w_in, lambda_q1, lambda_k1, lambda_q2, lambda_k2, subln_g,
              hg_lower_bounds, hg_norm_g, w_att_out, w_hg_out, w_out, g_post_mix,
              g_pre_ffn, w_up, w_down, g_post_ffn):
    f32 = jnp.float32
    sm = jax.nn.softmax(hg_lower_bounds.astype(f32), axis=0)
    lbs = jnp.cumsum(sm, axis=0) - sm[0:1]
    b_p, t_p = x_prompt.shape[0], x_prompt.shape[1]
    b_s, t_s = x_sample.shape[0], x_sample.shape[1]
    past_len = page_table.shape[1] * cache_k.shape[2]
    pos_p = jnp.arange(t_p)
    pos_s = past_len + jnp.arange(t_s)
    xp, xs = x_prompt, x_sample
    kp_l, vp_l, sp_l, ks_l, vs_l, ss_l = [], [], [], [], [], []
    for l in range(DEPTH):
        p = {'g_pre_mix': g_pre_mix[l], 'w_in': w_in[l],
             'lambda_q1': lambda_q1[l], 'lambda_k1': lambda_k1[l],
             'lambda_q2': lambda_q2[l], 'lambda_k2': lambda_k2[l],
             'subln_g': subln_g[l], 'lb': lbs[l], 'hg_norm_g': hg_norm_g[l],
             'w_att_out': w_att_out[l], 'w_hg_out': w_hg_out[l], 'w_out': w_out[l],
             'g_post_mix': g_post_mix[l], 'g_pre_ffn': g_pre_ffn[l],
             'w_up': w_up[l], 'w_down': w_down[l], 'g_post_ffn': g_post_ffn[l]}
        s0 = jnp.zeros((b_p, HG_H, HG_K, HG_V), f32)
        xp, kr, vr, S = _layer(xp, pos_p, l, s0, _diff_attn_prompt, p)
        kp_l.append(kr); vp_l.append(vr); sp_l.append(S)
        k_past = cache_k[l][page_table].reshape(b_s, past_len, ATT_H, 2 * ATT_HD)
        v_past = cache_v[l][page_table].reshape(b_s, past_len, ATT_H, ATT_VD)
        attend = functools.partial(_diff_attn_sample, k_past=k_past, v_past=v_past)
        xs, kr, vr, S = _layer(xs, pos_s, l, state_hgrn[l], attend, p)
        ks_l.append(kr); vs_l.append(vr); ss_l.append(S)
    return (xp, xs, jnp.stack(kp_l), jnp.stack(vp_l), jnp.stack(sp_l),
            jnp.stack(ks_l), jnp.stack(vs_l), jnp.stack(ss_l))
```

```python
import functools
import math

import numpy as np
import jax
import jax.numpy as jnp
from jax import lax
from jax.experimental import pallas as pl
from jax.experimental.pallas import tpu as pltpu

F32 = jnp.float32
BF16 = jnp.bfloat16

N_HEADS = 8
HEAD_W = 128
ATT_HD = 64
ROT_DIM = ATT_HD // 4
ROPE_THETA = 500000.0
PAGE_SIZE = 128
GLA_CHUNK = 64
GLA_SUB = 16
EPS = 1e-6
NEG_BIG = -1e30
GROUP_W = N_HEADS * HEAD_W

VMEM_LIMIT_BYTES = 56 * 1024 * 1024


def _lambda_init(l):
    return 0.8 - 0.6 * math.exp(-0.3 * l)


def _tile(n, pref):
    return pref if n % pref == 0 else n


def _cparams(*sem):
    return pltpu.CompilerParams(dimension_semantics=sem, vmem_limit_bytes=VMEM_LIMIT_BYTES)


def _rms(x, g):
    return x * lax.rsqrt(jnp.mean(x * x, axis=-1, keepdims=True) + EPS) * g


def _norm_cast_kernel(x_ref, g_ref, o_ref):
    o_ref[...] = _rms(x_ref[...], g_ref[...]).astype(o_ref.dtype)


def _norm_cast(x, g, l):
    m, d = x.shape
    tm = _tile(m, 512)
    return pl.pallas_call(
        _norm_cast_kernel,
        grid=(m // tm,),
        in_specs=[pl.BlockSpec((tm, d), lambda i: (i, 0)),
                  pl.BlockSpec((None, 1, d), lambda i: (l, 0, 0))],
        out_specs=pl.BlockSpec((tm, d), lambda i: (i, 0)),
        out_shape=jax.ShapeDtypeStruct((m, d), BF16),
        compiler_params=_cparams("parallel"),
        name="norm_cast",
    )(x, g)


def _proj_att_kernel(h_ref, w_ref, cos_ref, sa_ref, sb_ref,
                     q_ref, kf_ref, kb_ref, vf_ref, vb_ref, *, q_scale):
    j = pl.program_id(1)
    acc = jnp.dot(h_ref[...], w_ref[...], preferred_element_type=F32)

    def rope(c):
        x = acc[:, c * HEAD_W:(c + 1) * HEAD_W]
        return (x * cos_ref[...] + pltpu.roll(x, HEAD_W - ROT_DIM // 2, 1) * sa_ref[...]
                + pltpu.roll(x, ROT_DIM // 2, 1) * sb_ref[...])

    @pl.when(j == 0)
    def _():
        for c in range(N_HEADS):
            q_ref[:, c * HEAD_W:(c + 1) * HEAD_W] = (rope(c) * q_scale).astype(BF16)

    @pl.when(j == 1)
    def _():
        for c in range(N_HEADS):
            y = rope(c)
            kf_ref[:, c * HEAD_W:(c + 1) * HEAD_W] = y
            kb_ref[:, c * HEAD_W:(c + 1) * HEAD_W] = y.astype(BF16)

    @pl.when(j == 2)
    def _():
        vf_ref[...] = acc
        vb_ref[...] = acc.astype(BF16)


def _proj_att(h, w_in, l, rope_tabs):
    m, d = h.shape
    cos, sa, sb = rope_tabs
    tm = _tile(cos.shape[0], 512)
    nblk = cos.shape[0] // tm
    tab = pl.BlockSpec((tm, HEAD_W), lambda i, j: (i % nblk, 0))
    out = pl.BlockSpec((tm, GROUP_W), lambda i, j: (i, 0))
    return pl.pallas_call(
        functools.partial(_proj_att_kernel, q_scale=ATT_HD ** -0.5),
        grid=(m // tm, 3),
        in_specs=[pl.BlockSpec((tm, d), lambda i, j: (i, 0)),
                  pl.BlockSpec((None, d, GROUP_W), lambda i, j: (l, 0, j)),
                  tab, tab, tab],
        out_specs=[out] * 5,
        out_shape=[jax.ShapeDtypeStruct((m, GROUP_W), BF16),
                   jax.ShapeDtypeStruct((m, GROUP_W), F32),
                   jax.ShapeDtypeStruct((m, GROUP_W), BF16),
                   jax.ShapeDtypeStruct((m, GROUP_W), F32),
                   jax.ShapeDtypeStruct((m, GROUP_W), BF16)],
        compiler_params=_cparams("parallel", "arbitrary"),
        name="proj_att",
    )(h, w_in, cos, sa, sb)


def _proj_hg_kernel(h_ref, w_ref, lbp_ref, q_ref, k_ref, g_ref, v_ref, og_ref, *, layer):
    j = pl.program_id(1)
    acc = jnp.dot(h_ref[...], w_ref[...], preferred_element_type=F32)

    @pl.when(j == 0)
    def _():
        q_ref[...] = acc * jax.nn.sigmoid(acc) * (HEAD_W ** -0.5)

    @pl.when(j == 1)
    def _():
        if layer == 0:
            gval = jax.nn.sigmoid(acc)
            logg = jnp.minimum(acc, 0.0) - jnp.log1p(jnp.exp(-jnp.abs(acc)))
        else:
            p = lbp_ref[...]
            e = jnp.exp(p - jnp.max(p, axis=0, keepdims=True))
            sm = e / jnp.sum(e, axis=0, keepdims=True)
            lb = jnp.sum(sm[:layer + 1], axis=0, keepdims=True) - sm[0:1]
            gval = lb + (1.0 - lb) * jax.nn.sigmoid(acc)
            logg = jnp.log(gval)
        k_ref[...] = 1.0 - gval
        g_ref[...] = logg

    @pl.when(j == 2)
    def _():
        v_ref[...] = acc.astype(BF16)

    @pl.when(j == 3)
    def _():
        og_ref[...] = acc * jax.nn.sigmoid(acc)


def _proj_hg(h, w_in, lbp, l):
    m, d = h.shape
    tm = _tile(m, 512)
    depth = lbp.shape[0]
    out = pl.BlockSpec((tm, GROUP_W), lambda i, j: (i, 0))
    return pl.pallas_call(
        functools.partial(_proj_hg_kernel, layer=l),
        grid=(m // tm, 4),
        in_specs=[pl.BlockSpec((tm, d), lambda i, j: (i, 0)),
                  pl.BlockSpec((None, d, GROUP_W), lambda i, j: (l, 0, 3 + j)),
                  pl.BlockSpec((depth, GROUP_W), lambda i, j: (0, 0))],
        out_specs=[out] * 5,
        out_shape=[jax.ShapeDtypeStruct((m, GROUP_W), F32),
                   jax.ShapeDtypeStruct((m, GROUP_W), F32),
                   jax.ShapeDtypeStruct((m, GROUP_W), F32),
                   jax.ShapeDtypeStruct((m, GROUP_W), BF16),
                   jax.ShapeDtypeStruct((m, GROUP_W), F32)],
        compiler_params=_cparams("parallel", "arbitrary"),
        name="proj_hg",
    )(h, w_in, lbp)


def _proj_gate_kernel(h_ref, w_ref, o_ref):
    acc = jnp.dot(h_ref[...], w_ref[...], preferred_element_type=F32)
    o_ref[...] = jax.nn.sigmoid(acc).astype(o_ref.dtype)


def _proj_gate(h, w_in, l):
    m, d = h.shape
    tm = _tile(m, 1024)
    tn = _tile(2 * d, 1024)
    col0 = 7 * GROUP_W // tn
    return pl.pallas_call(
        _proj_gate_kernel,
        grid=(m // tm, 2 * d // tn),
        in_specs=[pl.BlockSpec((tm, d), lambda i, j: (i, 0)),
                  pl.BlockSpec((None, d, tn), lambda i, j: (l, 0, col0 + j))],
        out_specs=pl.BlockSpec((tm, tn), lambda i, j: (i, j)),
        out_shape=jax.ShapeDtypeStruct((m, 2 * d), BF16),
        compiler_params=_cparams("parallel", "arbitrary"),
        name="proj_gate",
    )(h, w_in)


def _lambda_value(lq1, lk1, lq2, lk2, lam_init):
    a = jnp.sum(lq1[...] * lk1[...], axis=1, keepdims=True)
    b = jnp.sum(lq2[...] * lk2[...], axis=1, keepdims=True)
    return jnp.exp(a) - jnp.exp(b) + lam_init


def _online_update(s, v, m, l, acc):
    m_new = jnp.maximum(m, jnp.max(s, axis=1, keepdims=True))
    alpha = jnp.exp(m - m_new)
    p = jnp.exp(s - m_new)
    l = alpha * l + jnp.sum(p, axis=1, keepdims=True)
    acc = alpha * acc + jnp.dot(p.astype(BF16), v, preferred_element_type=F32)
    return m_new, l, acc


def _flash_kernel(q_ref, k_ref, v_ref, lq1, lk1, lq2, lk2, sg_ref, o_ref, *, tq, lam_init):
    qi = pl.program_id(2)
    q = q_ref[...]
    lane = lax.broadcasted_iota(jnp.int32, q.shape, 1)
    q1 = jnp.where(lane < ATT_HD, q, jnp.zeros_like(q))
    q2 = jnp.where(lane >= ATT_HD, q, jnp.zeros_like(q))
    nt = (((1,), (1,)), ((), ()))

    def step(j, carry, masked):
        m1, l1, a1, m2, l2, a2 = carry
        r0 = pl.multiple_of(j * tq, tq)
        kt = k_ref[pl.ds(r0, tq), :]
        vt = v_ref[pl.ds(r0, tq), :]
        s1 = lax.dot_general(q1, kt, nt, preferred_element_type=F32)
        s2 = lax.dot_general(q2, kt, nt, preferred_element_type=F32)
        if masked:
            row = lax.broadcasted_iota(jnp.int32, s1.shape, 0)
            col = lax.broadcasted_iota(jnp.int32, s1.shape, 1)
            s1 = jnp.where(col <= row, s1, NEG_BIG)
            s2 = jnp.where(col <= row, s2, NEG_BIG)
        m1, l1, a1 = _online_update(s1, vt, m1, l1, a1)
        m2, l2, a2 = _online_update(s2, vt, m2, l2, a2)
        return m1, l1, a1, m2, l2, a2

    m0 = jnp.full((tq, 1), NEG_BIG, F32)
    l0 = jnp.zeros((tq, 1), F32)
    a0 = jnp.zeros((tq, HEAD_W), F32)
    carry = lax.fori_loop(0, qi, lambda j, c: step(j, c, False), (m0, l0, a0, m0, l0, a0))
    m1, l1, a1, m2, l2, a2 = step(qi, carry, True)
    lam = _lambda_value(lq1, lk1, lq2, lk2, lam_init)
    att = a1 / l1 - lam * (a2 / l2)
    o_ref[...] = (_rms(att, sg_ref[...]) * (1.0 - lam_init)).astype(o_ref.dtype)


def _flash(q, k, v, lams, sg, l, batch, seq):
    tq = _tile(seq, 512)
    nq = seq // tq
    lam_spec = pl.BlockSpec((None, 1, ATT_HD), lambda b, h, i: (l, 0, 0))
    return pl.pallas_call(
        functools.partial(_flash_kernel, tq=tq, lam_init=_lambda_init(l)),
        grid=(batch, N_HEADS, nq),
        in_specs=[pl.BlockSpec((tq, HEAD_W), lambda b, h, i: (b * nq + i, h)),
                  pl.BlockSpec((seq, HEAD_W), lambda b, h, i: (b, h)),
                  pl.BlockSpec((seq, HEAD_W), lambda b, h, i: (b, h)),
                  lam_spec, lam_spec, lam_spec, lam_spec,
                  pl.BlockSpec((None, 1, HEAD_W), lambda b, h, i: (l, 0, 0))],
        out_specs=pl.BlockSpec((tq, HEAD_W), lambda b, h, i: (b * nq + i, h)),
        out_shape=jax.ShapeDtypeStruct((batch * seq, GROUP_W), BF16),
        compiler_params=_cparams("parallel", "parallel", "arbitrary"),
        name="flash_prompt",
    )(q, k, v, *lams, sg)


def _decode_kernel(pt_ref, qb_ref, *refs, pages_per_step, n_new, lam_init):
    del pt_ref
    pp = pages_per_step
    k_refs, v_refs = refs[:pp], refs[pp:2 * pp]
    (kn_ref, vn_ref, lq1, lk1, lq2, lk2, sg_ref, o_ref, m_scr, l_scr, acc_scr) = refs[2 * pp:]
    p = pl.program_id(1)
    nt = (((1,), (1,)), ((), ()))
    n_rows = qb_ref.shape[0]

    @pl.when(p == 0)
    def _():
        m_scr[...] = jnp.full(m_scr.shape, NEG_BIG, F32)
        l_scr[...] = jnp.zeros(l_scr.shape, F32)
        acc_scr[...] = jnp.zeros(acc_scr.shape, F32)

    qb = qb_ref[...]
    m, lsum, acc = m_scr[...], l_scr[...], acc_scr[...]
    for u in range(pp):
        s = lax.dot_general(qb, k_refs[u][...].astype(BF16), nt, preferred_element_type=F32)
        m, lsum, acc = _online_update(s, v_refs[u][...].astype(BF16), m, lsum, acc)
    m_scr[...], l_scr[...], acc_scr[...] = m, lsum, acc

    @pl.when(p == pl.num_programs(1) - 1)
    def _():
        s = lax.dot_general(qb, kn_ref[...], nt, preferred_element_type=F32)
        tok = lax.broadcasted_iota(jnp.int32, s.shape, 0) // (2 * N_HEADS)
        col = lax.broadcasted_iota(jnp.int32, s.shape, 1)
        s = jnp.where(col <= tok, s, NEG_BIG)
        m2, l2, acc2 = _online_update(s, vn_ref[...], m, lsum, acc)
        out = acc2 / l2
        lam = _lambda_value(lq1, lk1, lq2, lk2, lam_init)
        hm = lax.broadcasted_iota(jnp.int32, out.shape, 0) % (2 * N_HEADS)
        lane_head = lax.broadcasted_iota(jnp.int32, out.shape, 1) // HEAD_W
        coef = jnp.where(hm % 2 == 0, 1.0, -lam)
        out = jnp.where(lane_head == hm // 2, out * coef, 0.0)
        att = jnp.sum(out.reshape(n_new, 2 * N_HEADS, GROUP_W), axis=1)
        for h in range(N_HEADS):
            sl = slice(h * HEAD_W, (h + 1) * HEAD_W)
            o_ref[:, sl] = _rms(att[:, sl], sg_ref[...]) * (1.0 - lam_init)


def _decode_attn(qb, cache_k, cache_v, page_table, k_new, v_new, lams, sg, l):
    nb, n_rows, _ = qb.shape
    n_new = n_rows // (2 * N_HEADS)
    n_pages = page_table.shape[1]
    pp = 8 if n_pages % 8 == 0 else 1
    pt_flat = page_table.reshape(-1)

    def page_spec(u):
        return pl.BlockSpec((None, None, PAGE_SIZE, GROUP_W),
                            lambda b, p, pt: (l, pt[b * n_pages + p * pp + u], 0, 0))

    lam_spec = pl.BlockSpec((None, 1, ATT_HD), lambda b, p, pt: (l, 0, 0))
    new_spec = pl.BlockSpec((None, 16, GROUP_W), lambda b, p, pt: (b, 0, 0))
    grid_spec = pltpu.PrefetchScalarGridSpec(
        num_scalar_prefetch=1,
        grid=(nb, n_pages // pp),
        in_specs=([pl.BlockSpec((None, n_rows, GROUP_W), lambda b, p, pt: (b, 0, 0))]
                  + [page_spec(u) for u in range(pp)] * 2
                  + [new_spec, new_spec, lam_spec, lam_spec, lam_spec, lam_spec,
                     pl.BlockSpec((None, 1, HEAD_W), lambda b, p, pt: (l, 0, 0))]),
        out_specs=pl.BlockSpec((None, n_new, GROUP_W), lambda b, p, pt: (b, 0, 0)),
        scratch_shapes=[pltpu.VMEM((n_rows, 1), F32), pltpu.VMEM((n_rows, 1), F32),
                        pltpu.VMEM((n_rows, GROUP_W), F32)],
    )
    return pl.pallas_call(
        functools.partial(_decode_kernel, pages_per_step=pp, n_new=n_new,
                          lam_init=_lambda_init(l)),
        grid_spec=grid_spec,
        out_shape=jax.ShapeDtypeStruct((nb, n_new, GROUP_W), F32),
        compiler_params=_cparams("parallel", "arbitrary"),
        name="decode_attn",
    )(pt_flat, qb, *([cache_k] * pp), *([cache_v] * pp), k_new, v_new, *lams, sg)


def _gla_kernel(*refs, chunk, sub, has_s0):
    if has_s0:
        (q_ref, k_ref, g_ref, v_ref, og_ref, ng_ref, tril_ref, bsel_ref, s0_ref,
         o_ref, sout_ref, st_scr) = refs
    else:
        (q_ref, k_ref, g_ref, v_ref, og_ref, ng_ref, tril_ref, bsel_ref,
         o_ref, sout_ref, st_scr) = refs
    t = pl.program_id(2)
    n_chunks = q_ref.shape[0] // chunk
    n_sub = chunk // sub
    nt = (((1,), (1,)), ((), ()))
    tn = (((0,), (0,)), ((), ()))

    @pl.when(t == 0)
    def _():
        if has_s0:
            st_scr[...] = s0_ref[...].T
        else:
            st_scr[...] = jnp.zeros(st_scr.shape, F32)

    tril = tril_ref[...]
    bsel = bsel_ref[...]
    row_c = lax.broadcasted_iota(jnp.int32, (chunk, HEAD_W), 0)
    row_s = lax.broadcasted_iota(jnp.int32, (sub, HEAD_W), 0)
    col_a = lax.broadcasted_iota(jnp.int32, (sub, chunk), 1)

    def one_chunk(c, _):
        r0 = pl.multiple_of(c * chunk, chunk)
        q = q_ref[pl.ds(r0, chunk), :]
        k = k_ref[pl.ds(r0, chunk), :]
        v = v_ref[pl.ds(r0, chunk), :]
        b = jnp.dot(tril, g_ref[pl.ds(r0, chunk), :], precision=lax.Precision.HIGHEST,
                    preferred_element_type=F32)
        st = st_scr[...]
        o = lax.dot_general((q * jnp.exp(b)).astype(BF16), st.astype(BF16), nt,
                            preferred_element_type=F32)
        a_rows = []
        for i in range(n_sub):
            rs = slice(i * sub, (i + 1) * sub)
            b_i, q_i, k_i = b[rs], q[rs], k[rs]
            blocks = []
            for s in range(sub):
                d = jnp.exp(b_i - b_i[s:s + 1]) * (q_i * k_i[s:s + 1])
                blocks.append(jnp.where(row_s >= s, d, 0.0).astype(BF16))
            e = jnp.concatenate(blocks, axis=1)
            a = jnp.dot(e, bsel, preferred_element_type=F32)
            a = jnp.where(col_a // sub == i, a, 0.0)
            if i > 0:
                r = b[i * sub - 1:i * sub]
                qt = q_i * jnp.exp(b_i - r)
                kt = jnp.where(row_c < i * sub, k * jnp.exp(jnp.minimum(r - b, 0.0)), 0.0)
                a = a + lax.dot_general(qt.astype(BF16), kt.astype(BF16), nt,
                                        preferred_element_type=F32)
            a_rows.append(a)
        a_full = a_rows[0] if n_sub == 1 else jnp.concatenate(a_rows, axis=0)
        o = o + jnp.dot(a_full.astype(BF16), v, preferred_element_type=F32)
        b_end = b[chunk - 1:chunk]
        kd = (k * jnp.exp(b_end - b)).astype(BF16)
        st_scr[...] = st * jnp.exp(b_end) + lax.dot_general(v, kd, tn, preferred_element_type=F32)
        o_ref[pl.ds(r0, chunk), :] = (_rms(o, ng_ref[...]) * og_ref[pl.ds(r0, chunk), :]
                                      ).astype(o_ref.dtype)
        return 0

    lax.fori_loop(0, n_chunks, one_chunk, 0)

    @pl.when(t == pl.num_programs(2) - 1)
    def _():
        sout_ref[...] = st_scr[...].T


def _gla_constants(chunk, sub):
    tril = np.tril(np.ones((chunk, chunk), np.float32))
    bsel = np.zeros((sub * HEAD_W, chunk), np.float32)
    for s in range(sub):
        for j in range(chunk // sub):
            bsel[s * HEAD_W:(s + 1) * HEAD_W, j * sub + s] = 1.0
    return jnp.asarray(tril), jnp.asarray(bsel, dtype=BF16)


def _gla(q, k, g, v, og, ng, s0, l, batch, seq, chunk, sub):
    tb = _tile(seq, 512)
    nt_ = seq // tb
    tril, bsel = _gla_constants(chunk, sub)
    row = pl.BlockSpec((tb, HEAD_W), lambda b, h, t: (b * nt_ + t, h))
    st_spec = pl.BlockSpec((None, None, HEAD_W, HEAD_W), lambda b, h, t: (b, h, 0, 0))
    in_specs = [row, row, row, row, row,
                pl.BlockSpec((None, 1, HEAD_W), lambda b, h, t: (l, 0, 0)),
                pl.BlockSpec(tril.shape, lambda b, h, t: (0, 0)),
                pl.BlockSpec(bsel.shape, lambda b, h, t: (0, 0))]
    args = [q, k, g, v, og, ng, tril, bsel]
    if s0 is not None:
        in_specs.append(pl.BlockSpec((None, None, None, HEAD_W, HEAD_W),
                                     lambda b, h, t: (l, b, h, 0, 0)))
        args.append(s0)
    return pl.pallas_call(
        functools.partial(_gla_kernel, chunk=chunk, sub=sub, has_s0=s0 is not None),
        grid=(batch, N_HEADS, nt_),
        in_specs=in_specs,
        out_specs=[row, st_spec],
        out_shape=[jax.ShapeDtypeStruct((batch * seq, GROUP_W), BF16),
                   jax.ShapeDtypeStruct((batch, N_HEADS, HEAD_W, HEAD_W), F32)],
        scratch_shapes=[pltpu.VMEM((HEAD_W, HEAD_W), F32)],
        compiler_params=_cparams("parallel", "parallel", "arbitrary"),
        name="gla",
    )(*args)


def _merge_kernel(att_ref, o_ref, wa_ref, wh_ref, ga_ref, gb_ref, out_ref):
    a = jnp.dot(att_ref[...], wa_ref[...], preferred_element_type=F32)
    b = jnp.dot(o_ref[...], wh_ref[...], preferred_element_type=F32)
    out_ref[...] = (ga_ref[...].astype(F32) * a + gb_ref[...].astype(F32) * b).astype(out_ref.dtype)


def _merge(att, o, w_att, w_hg, gates, l):
    m = att.shape[0]
    d = w_att.shape[2]
    tm = _tile(m, 1024)
    tn = _tile(d, 1024)
    nj = d // tn
    return pl.pallas_call(
        _merge_kernel,
        grid=(m // tm, nj),
        in_specs=[pl.BlockSpec((tm, GROUP_W), lambda i, j: (i, 0)),
                  pl.BlockSpec((tm, GROUP_W), lambda i, j: (i, 0)),
                  pl.BlockSpec((None, GROUP_W, tn), lambda i, j: (l, 0, j)),
                  pl.BlockSpec((None, GROUP_W, tn), lambda i, j: (l, 0, j)),
                  pl.BlockSpec((tm, tn), lambda i, j: (i, j)),
                  pl.BlockSpec((tm, tn), lambda i, j: (i, nj + j))],
        out_specs=pl.BlockSpec((tm, tn), lambda i, j: (i, j)),
        out_shape=jax.ShapeDtypeStruct((m, d), BF16),
        compiler_params=_cparams("parallel", "arbitrary"),
        name="merge",
    )(att, o, w_att, w_hg, gates, gates)


def _out_proj_kernel(mix_ref, w_ref, x_ref, g_ref, o_ref):
    y = jnp.dot(mix_ref[...], w_ref[...], preferred_element_type=F32)
    o_ref[...] = x_ref[...] + _rms(y, g_ref[...])


def _out_proj(mixed, w_out, x, g, l):
    m, d = x.shape
    tm = _tile(m, 512)
    return pl.pallas_call(
        _out_proj_kernel,
        grid=(m // tm,),
        in_specs=[pl.BlockSpec((tm, d), lambda i: (i, 0)),
                  pl.BlockSpec((None, d, d), lambda i: (l, 0, 0)),
                  pl.BlockSpec((tm, d), lambda i: (i, 0)),
                  pl.BlockSpec((None, 1, d), lambda i: (l, 0, 0))],
        out_specs=pl.BlockSpec((tm, d), lambda i: (i, 0)),
        out_shape=jax.ShapeDtypeStruct((m, d), F32),
        compiler_params=_cparams("parallel"),
        name="out_proj",
    )(mixed, w_out, x, g)


def _ffn_kernel(x_ref, g1_ref, wu_ref, wd_ref, g2_ref, o_ref, h_scr):
    f = pl.program_id(1)

    @pl.when(f == 0)
    def _():
        h_scr[...] = _rms(x_ref[...], g1_ref[...]).astype(BF16)
        o_ref[...] = jnp.zeros(o_ref.shape, F32)

    u = jnp.dot(h_scr[...], wu_ref[...], preferred_element_type=F32)
    u = jnp.square(jnp.maximum(u, 0.0)).astype(BF16)
    o_ref[...] += jnp.dot(u, wd_ref[...], preferred_element_type=F32)

    @pl.when(f == pl.num_programs(1) - 1)
    def _():
        o_ref[...] = x_ref[...] + _rms(o_ref[...], g2_ref[...])


def _ffn(x, g1, w_up, w_down, g2, l):
    m, d = x.shape
    dff = w_up.shape[2]
    tm = _tile(m, 512)
    tf = _tile(dff, 1024)
    return pl.pallas_call(
        _ffn_kernel,
        grid=(m // tm, dff // tf),
        in_specs=[pl.BlockSpec((tm, d), lambda i, f: (i, 0)),
                  pl.BlockSpec((None, 1, d), lambda i, f: (l, 0, 0)),
                  pl.BlockSpec((None, d, tf), lambda i, f: (l, 0, f)),
                  pl.BlockSpec((None, tf, d), lambda i, f: (l, f, 0)),
                  pl.BlockSpec((None, 1, d), lambda i, f: (l, 0, 0))],
        out_specs=pl.BlockSpec((tm, d), lambda i, f: (i, 0)),
        out_shape=jax.ShapeDtypeStruct((m, d), F32),
        scratch_shapes=[pltpu.VMEM((tm, d), BF16)],
        compiler_params=_cparams("parallel", "arbitrary"),
        name="ffn",
    )(x, g1, w_up, w_down, g2)


def _rope_tables(pos):
    half = ROT_DIM // 2
    inv = ROPE_THETA ** (-jnp.arange(0, ROT_DIM, 2, dtype=F32) / ROT_DIM)
    ang = pos.astype(F32)[:, None] * inv[None, :]
    cos, sin = jnp.cos(ang), jnp.sin(ang)
    n = pos.shape[0]
    one = jnp.ones((n, ATT_HD - ROT_DIM), F32)
    zero = jnp.zeros((n, ATT_HD - ROT_DIM), F32)
    zh = jnp.zeros((n, half), F32)
    cos64 = jnp.concatenate([cos, cos, one], axis=1)
    sa64 = jnp.concatenate([-sin, zh, zero], axis=1)
    sb64 = jnp.concatenate([zh, sin, zero], axis=1)
    return tuple(jnp.concatenate([t, t], axis=1) for t in (cos64, sa64, sb64))


def _mix_inputs(x, wts, l, rope_tabs):
    h = _norm_cast(x, wts["g_pre_mix"], l)
    q, kf, kb, vf, vb = _proj_att(h, wts["w_in"], l, rope_tabs)
    hq, hk, hg, hv, og = _proj_hg(h, wts["w_in"], wts["hg_lower_bounds"], l)
    gates = _proj_gate(h, wts["w_in"], l)
    return q, kf, kb, vf, vb, hq, hk, hg, hv, og, gates


def _mix_outputs(x, att, o, gates, wts, l):
    mixed = _merge(att, o, wts["w_att_out"], wts["w_hg_out"], gates, l)
    x = _out_proj(mixed, wts["w_out"], x, wts["g_post_mix"], l)
    return _ffn(x, wts["g_pre_ffn"], wts["w_up"], wts["w_down"], wts["g_post_ffn"], l)


def kernel(x_prompt, x_sample, cache_k, cache_v, state_hgrn, page_table, g_pre_mix, w_in,
           lambda_q1, lambda_k1, lambda_q2, lambda_k2, subln_g, hg_lower_bounds, hg_norm_g,
           w_att_out, w_hg_out, w_out, g_post_mix, g_pre_ffn, w_up, w_down, g_post_ffn):
    depth = w_in.shape[0]
    b_p, t_p, d = x_prompt.shape
    b_s, t_s, _ = x_sample.shape
    n_pool = cache_k.shape[1]
    past_len = page_table.shape[1] * PAGE_SIZE

    def vec(a):
        return a.reshape(depth, 1, a.shape[-1])

    wts = {
        "g_pre_mix": vec(g_pre_mix), "g_post_mix": vec(g_post_mix),
        "g_pre_ffn": vec(g_pre_ffn), "g_post_ffn": vec(g_post_ffn),
        "hg_lower_bounds": hg_lower_bounds,
        "w_in": w_in.astype(BF16), "w_att_out": w_att_out.astype(BF16),
        "w_hg_out": w_hg_out.astype(BF16), "w_out": w_out.astype(BF16),
        "w_up": w_up.astype(BF16), "w_down": w_down.astype(BF16),
    }
    lams = [vec(a) for a in (lambda_q1, lambda_k1, lambda_q2, lambda_k2)]
    sg, ng = vec(subln_g), vec(hg_norm_g)
    ck = cache_k.reshape(depth, n_pool, PAGE_SIZE, GROUP_W)
    cv = cache_v.reshape(depth, n_pool, PAGE_SIZE, GROUP_W)

    tabs_p = _rope_tables(jnp.arange(t_p))
    tabs_s = _rope_tables(past_len + (jnp.arange(b_s * t_s) % t_s))
    chunk_p = GLA_CHUNK if t_p % GLA_CHUNK == 0 else t_p
    sub_p = GLA_SUB if chunk_p % GLA_SUB == 0 else chunk_p
    t_pad = 16
    hm_mask = (jnp.arange(GROUP_W)[None, :] // ATT_HD == jnp.arange(2 * N_HEADS)[:, None])

    xp = x_prompt.reshape(b_p * t_p, d)
    xs = x_sample.reshape(b_s * t_s, d)
    outs = [[] for _ in range(6)]
    for l in range(depth):
        q, kf, kb, vf, vb, hq, hk, hg, hv, og, gates = _mix_inputs(xp, wts, l, tabs_p)
        att = _flash(q, kb, vb, lams, sg, l, b_p, t_p)
        o, s_fin = _gla(hq, hk, hg, hv, og, ng, None, l, b_p, t_p, chunk_p, sub_p)
        xp = _mix_outputs(xp, att, o, gates, wts, l)
        outs[0].append(kf.reshape(b_p, t_p, N_HEADS, HEAD_W))
        outs[1].append(vf.reshape(b_p, t_p, N_HEADS, HEAD_W))
        outs[2].append(s_fin)

        q, kf, kb, vf, vb, hq, hk, hg, hv, og, gates = _mix_inputs(xs, wts, l, tabs_s)
        qb = jnp.where(hm_mask[None, None], q.reshape(b_s, t_s, 1, GROUP_W), jnp.zeros((), BF16))
        qb = qb.reshape(b_s, t_s * 2 * N_HEADS, GROUP_W)
        pad16 = lambda a: jnp.pad(a.reshape(b_s, t_s, GROUP_W), ((0, 0), (0, 16 - t_s), (0, 0)))
        att = _decode_attn(qb, ck, cv, page_table, pad16(kb), pad16(vb), lams, sg, l)
        att = att.reshape(b_s * t_s, GROUP_W).astype(BF16)
        pad8 = lambda a: jnp.pad(a.reshape(b_s, t_s, GROUP_W),
                                 ((0, 0), (0, t_pad - t_s), (0, 0))).reshape(b_s * t_pad, GROUP_W)
        o, s_fin = _gla(pad8(hq), pad8(hk), pad8(hg), pad8(hv), pad8(og), ng, state_hgrn, l,
                        b_s, t_pad, t_pad, t_pad)
        o = o.reshape(b_s, t_pad, GROUP_W)[:, :t_s].reshape(b_s * t_s, GROUP_W)
        xs = _mix_outputs(xs, att, o, gates, wts, l)
        outs[3].append(kf.reshape(b_s, t_s, N_HEADS, HEAD_W))
        outs[4].append(vf.reshape(b_s, t_s, N_HEADS, HEAD_W))
        outs[5].append(s_fin)

    return (xp.reshape(b_p, t_p, d), xs.reshape(b_s, t_s, d),
            jnp.stack(outs[0]), jnp.stack(outs[1]), jnp.stack(outs[2]),
            jnp.stack(outs[3]), jnp.stack(outs[4]), jnp.stack(outs[5]))
```

```python
import functools
import math

import numpy as np
import jax
import jax.numpy as jnp
from jax import lax
from jax.experimental import pallas as pl
from jax.experimental.pallas import tpu as pltpu

F32 = jnp.float32
BF16 = jnp.bfloat16

N_HEADS = 8
HEAD_W = 128
ATT_HD = 64
ROT_DIM = ATT_HD // 4
ROPE_THETA = 500000.0
PAGE_SIZE = 128
PAGE_ROWS = PAGE_SIZE * N_HEADS
GLA_CHUNK = 64
GLA_DIAG = 8
GLA_ROWS = 256
EPS = 1e-6
NEG_BIG = -1e30
LOG2_E = math.log2(math.e)
GROUP_W = N_HEADS * HEAD_W

VMEM_LIMIT_BYTES = 56 * 1024 * 1024

NT_DIMS = (((1,), (1,)), ((), ()))
TN_DIMS = (((0,), (0,)), ((), ()))


def _lambda_init(l):
    return 0.8 - 0.6 * math.exp(-0.3 * l)


def _tile(n, pref):
    return pref if n % pref == 0 else n


def _cparams(*sem):
    return pltpu.CompilerParams(dimension_semantics=sem, vmem_limit_bytes=VMEM_LIMIT_BYTES)


def _rms(x, g):
    return x * lax.rsqrt(jnp.mean(x * x, axis=-1, keepdims=True) + EPS) * g


def _norm_cast_kernel(x_ref, g_ref, o_ref):
    o_ref[...] = _rms(x_ref[...], g_ref[...]).astype(o_ref.dtype)


def _norm_cast(x, g, l):
    m, d = x.shape
    tm = _tile(m, 512)
    return pl.pallas_call(
        _norm_cast_kernel,
        grid=(m // tm,),
        in_specs=[pl.BlockSpec((tm, d), lambda i: (i, 0)),
                  pl.BlockSpec((None, 1, d), lambda i: (l, 0, 0))],
        out_specs=pl.BlockSpec((tm, d), lambda i: (i, 0)),
        out_shape=jax.ShapeDtypeStruct((m, d), BF16),
        compiler_params=_cparams("parallel"),
        name="norm_cast",
    )(x, g)


def _proj_att_kernel(h_ref, w_ref, cos_ref, sa_ref, sb_ref,
                     q_ref, kf_ref, kb_ref, vf_ref, vb_ref, *, q_scale):
    j = pl.program_id(1)
    acc = jnp.dot(h_ref[...], w_ref[...], preferred_element_type=F32)

    def rope(c):
        x = acc[:, c * HEAD_W:(c + 1) * HEAD_W]
        return (x * cos_ref[...] + pltpu.roll(x, HEAD_W - ROT_DIM // 2, 1) * sa_ref[...]
                + pltpu.roll(x, ROT_DIM // 2, 1) * sb_ref[...])

    @pl.when(j == 0)
    def _():
        for c in range(N_HEADS):
            q_ref[:, c * HEAD_W:(c + 1) * HEAD_W] = (rope(c) * q_scale).astype(BF16)

    @pl.when(j == 1)
    def _():
        for c in range(N_HEADS):
            y = rope(c)
            kf_ref[:, c * HEAD_W:(c + 1) * HEAD_W] = y
            kb_ref[:, c * HEAD_W:(c + 1) * HEAD_W] = y.astype(BF16)

    @pl.when(j == 2)
    def _():
        vf_ref[...] = acc
        vb_ref[...] = acc.astype(BF16)


def _proj_att(h, w_in, l, rope_tabs):
    m, d = h.shape
    cos, sa, sb = rope_tabs
    tm = _tile(cos.shape[0], 512)
    nblk = cos.shape[0] // tm
    tab = pl.BlockSpec((tm, HEAD_W), lambda i, j: (i % nblk, 0))
    out = pl.BlockSpec((tm, GROUP_W), lambda i, j: (i, 0))
    return pl.pallas_call(
        functools.partial(_proj_att_kernel, q_scale=ATT_HD ** -0.5 * LOG2_E),
        grid=(m // tm, 3),
        in_specs=[pl.BlockSpec((tm, d), lambda i, j: (i, 0)),
                  pl.BlockSpec((None, d, GROUP_W), lambda i, j: (l, 0, j)),
                  tab, tab, tab],
        out_specs=[out] * 5,
        out_shape=[jax.ShapeDtypeStruct((m, GROUP_W), BF16),
                   jax.ShapeDtypeStruct((m, GROUP_W), F32),
                   jax.ShapeDtypeStruct((m, GROUP_W), BF16),
                   jax.ShapeDtypeStruct((m, GROUP_W), F32),
                   jax.ShapeDtypeStruct((m, GROUP_W), BF16)],
        compiler_params=_cparams("parallel", "arbitrary"),
        name="proj_att",
    )(h, w_in, cos, sa, sb)


def _proj_hg_kernel(h_ref, w_ref, lbp_ref, q_ref, k_ref, g_ref, v_ref, og_ref, *, layer):
    j = pl.program_id(1)
    acc = jnp.dot(h_ref[...], w_ref[...], preferred_element_type=F32)

    @pl.when(j == 0)
    def _():
        q_ref[...] = acc * jax.nn.sigmoid(acc) * (HEAD_W ** -0.5)

    @pl.when(j == 1)
    def _():
        if layer == 0:
            gval = jax.nn.sigmoid(acc)
            logg = jnp.minimum(acc, 0.0) - jnp.log1p(jnp.exp(-jnp.abs(acc)))
        else:
            p = lbp_ref[...]
            e = jnp.exp(p - jnp.max(p, axis=0, keepdims=True))
            sm = e / jnp.sum(e, axis=0, keepdims=True)
            lb = jnp.sum(sm[:layer + 1], axis=0, keepdims=True) - sm[0:1]
            gval = lb + (1.0 - lb) * jax.nn.sigmoid(acc)
            logg = jnp.log(gval)
        k_ref[...] = 1.0 - gval
        g_ref[...] = logg

    @pl.when(j == 2)
    def _():
        v_ref[...] = acc.astype(BF16)

    @pl.when(j == 3)
    def _():
        og_ref[...] = acc * jax.nn.sigmoid(acc)


def _proj_hg(h, w_in, lbp, l):
    m, d = h.shape
    tm = _tile(m, 512)
    depth = lbp.shape[0]
    out = pl.BlockSpec((tm, GROUP_W), lambda i, j: (i, 0))
    return pl.pallas_call(
        functools.partial(_proj_hg_kernel, layer=l),
        grid=(m // tm, 4),
        in_specs=[pl.BlockSpec((tm, d), lambda i, j: (i, 0)),
                  pl.BlockSpec((None, d, GROUP_W), lambda i, j: (l, 0, 3 + j)),
                  pl.BlockSpec((depth, GROUP_W), lambda i, j: (0, 0))],
        out_specs=[out] * 5,
        out_shape=[jax.ShapeDtypeStruct((m, GROUP_W), F32),
                   jax.ShapeDtypeStruct((m, GROUP_W), F32),
                   jax.ShapeDtypeStruct((m, GROUP_W), F32),
                   jax.ShapeDtypeStruct((m, GROUP_W), BF16),
                   jax.ShapeDtypeStruct((m, GROUP_W), F32)],
        compiler_params=_cparams("parallel", "arbitrary"),
        name="proj_hg",
    )(h, w_in, lbp)


def _proj_gate_kernel(h_ref, w_ref, o_ref):
    acc = jnp.dot(h_ref[...], w_ref[...], preferred_element_type=F32)
    o_ref[...] = jax.nn.sigmoid(acc).astype(o_ref.dtype)


def _proj_gate(h, w_in, l):
    m, d = h.shape
    tm = _tile(m, 1024)
    tn = _tile(2 * d, 1024)
    col0 = 7 * GROUP_W // tn
    return pl.pallas_call(
        _proj_gate_kernel,
        grid=(m // tm, 2 * d // tn),
        in_specs=[pl.BlockSpec((tm, d), lambda i, j: (i, 0)),
                  pl.BlockSpec((None, d, tn), lambda i, j: (l, 0, col0 + j))],
        out_specs=pl.BlockSpec((tm, tn), lambda i, j: (i, j)),
        out_shape=jax.ShapeDtypeStruct((m, 2 * d), BF16),
        compiler_params=_cparams("parallel", "arbitrary"),
        name="proj_gate",
    )(h, w_in)


def _lambda_value(lq1, lk1, lq2, lk2, lam_init):
    a = jnp.sum(lq1[...] * lk1[...], axis=1, keepdims=True)
    b = jnp.sum(lq2[...] * lk2[...], axis=1, keepdims=True)
    return jnp.exp(a) - jnp.exp(b) + lam_init


def _online_update(s, v, m, l, acc):
    m_new = jnp.maximum(m, jnp.max(s, axis=1, keepdims=True))
    alpha = jnp.exp2(m - m_new)
    p = jnp.exp2(s - m_new)
    l = alpha * l + jnp.sum(p, axis=1, keepdims=True)
    acc = alpha * acc + jnp.dot(p.astype(BF16), v, preferred_element_type=F32)
    return m_new, l, acc


def _flash_kernel(q_ref, k_ref, v_ref, lq1, lk1, lq2, lk2, sg_ref, o_ref, *, tq, lam_init):
    qi = pl.program_id(2)
    q = q_ref[...]
    lane = lax.broadcasted_iota(jnp.int32, q.shape, 1)
    q1 = jnp.where(lane < ATT_HD, q, jnp.zeros_like(q))
    q2 = jnp.where(lane >= ATT_HD, q, jnp.zeros_like(q))

    def step(j, carry, masked):
        m1, l1, a1, m2, l2, a2 = carry
        r0 = pl.multiple_of(j * tq, tq)
        kt = k_ref[pl.ds(r0, tq), :]
        vt = v_ref[pl.ds(r0, tq), :]
        s1 = lax.dot_general(q1, kt, NT_DIMS, preferred_element_type=F32)
        s2 = lax.dot_general(q2, kt, NT_DIMS, preferred_element_type=F32)
        if masked:
            row = lax.broadcasted_iota(jnp.int32, s1.shape, 0)
            col = lax.broadcasted_iota(jnp.int32, s1.shape, 1)
            s1 = jnp.where(col <= row, s1, NEG_BIG)
            s2 = jnp.where(col <= row, s2, NEG_BIG)
        m1, l1, a1 = _online_update(s1, vt, m1, l1, a1)
        m2, l2, a2 = _online_update(s2, vt, m2, l2, a2)
        return m1, l1, a1, m2, l2, a2

    m0 = jnp.full((tq, 1), NEG_BIG, F32)
    l0 = jnp.zeros((tq, 1), F32)
    a0 = jnp.zeros((tq, HEAD_W), F32)
    carry = lax.fori_loop(0, qi, lambda j, c: step(j, c, False), (m0, l0, a0, m0, l0, a0))
    m1, l1, a1, m2, l2, a2 = step(qi, carry, True)
    lam = _lambda_value(lq1, lk1, lq2, lk2, lam_init)
    att = a1 / l1 - lam * (a2 / l2)
    o_ref[...] = (_rms(att, sg_ref[...]) * (1.0 - lam_init)).astype(o_ref.dtype)


def _flash(q, k, v, lams, sg, l, batch, seq):
    tq = _tile(seq, 512)
    nq = seq // tq
    lam_spec = pl.BlockSpec((None, 1, ATT_HD), lambda b, h, i: (l, 0, 0))
    return pl.pallas_call(
        functools.partial(_flash_kernel, tq=tq, lam_init=_lambda_init(l)),
        grid=(batch, N_HEADS, nq),
        in_specs=[pl.BlockSpec((tq, HEAD_W), lambda b, h, i: (b * nq + i, h)),
                  pl.BlockSpec((seq, HEAD_W), lambda b, h, i: (b, h)),
                  pl.BlockSpec((seq, HEAD_W), lambda b, h, i: (b, h)),
                  lam_spec, lam_spec, lam_spec, lam_spec,
                  pl.BlockSpec((None, 1, HEAD_W), lambda b, h, i: (l, 0, 0))],
        out_specs=pl.BlockSpec((tq, HEAD_W), lambda b, h, i: (b * nq + i, h)),
        out_shape=jax.ShapeDtypeStruct((batch * seq, GROUP_W), BF16),
        compiler_params=_cparams("parallel", "parallel", "arbitrary"),
        name="flash_prompt",
    )(q, k, v, *lams, sg)


def _decode_kernel(pt_ref, q_ref, *refs, pages_per_step, lam_init):
    del pt_ref
    pp = pages_per_step
    k_refs, v_refs = refs[:pp], refs[pp:2 * pp]
    (kn_ref, vn_ref, lq1, lk1, lq2, lk2, sg_ref, o_ref, m_scr, l_scr, acc_scr) = refs[2 * pp:]
    p = pl.program_id(1)
    n_rows = q_ref.shape[0]
    half = n_rows // 2

    @pl.when(p == 0)
    def _():
        m_scr[...] = jnp.full(m_scr.shape, NEG_BIG, F32)
        l_scr[...] = jnp.zeros(l_scr.shape, F32)
        acc_scr[...] = jnp.zeros(acc_scr.shape, F32)

    q = q_ref[...]

    def own_head(n_cols):
        r = lax.broadcasted_iota(jnp.int32, (n_rows, n_cols), 0)
        c = lax.broadcasted_iota(jnp.int32, (n_rows, n_cols), 1)
        return r % N_HEADS == c % N_HEADS, r, c

    valid, _, _ = own_head(PAGE_ROWS)
    scores = []
    m_prev = m_scr[...]
    m_new = m_prev
    for u in range(pp):
        s = lax.dot_general(q, k_refs[u][...].astype(BF16), NT_DIMS, preferred_element_type=F32)
        s = jnp.where(valid, s, NEG_BIG)
        scores.append(s)
        m_new = jnp.maximum(m_new, jnp.max(s, axis=1, keepdims=True))
    alpha = jnp.exp2(m_prev - m_new)
    lsum = alpha * l_scr[...]
    acc = alpha * acc_scr[...]
    for u in range(pp):
        pu = jnp.exp2(scores[u] - m_new)
        lsum = lsum + jnp.sum(pu, axis=1, keepdims=True)
        acc = acc + jnp.dot(pu.astype(BF16), v_refs[u][...].astype(BF16),
                            preferred_element_type=F32)
    m_scr[...], l_scr[...], acc_scr[...] = m_new, lsum, acc

    @pl.when(p == pl.num_programs(1) - 1)
    def _():
        n_new_rows = kn_ref.shape[0]
        s = lax.dot_general(q, kn_ref[...], NT_DIMS, preferred_element_type=F32)
        same_head, r, c = own_head(n_new_rows)
        causal = c // N_HEADS <= (r % half) // N_HEADS
        s = jnp.where(same_head & causal, s, NEG_BIG)
        _, l2, acc2 = _online_update(s, vn_ref[...], m_new, lsum, acc)
        out = acc2 / l2
        lam = _lambda_value(lq1, lk1, lq2, lk2, lam_init)
        att = out[:half] - lam * out[half:]
        o_ref[...] = _rms(att, sg_ref[...]) * (1.0 - lam_init)


def _decode_attn(q2, cache_k, cache_v, page_table, k_new, v_new, lams, sg, l):
    nb, n_rows, _ = q2.shape
    n_new_rows = k_new.shape[1]
    n_pages = page_table.shape[1]
    pp = 8 if n_pages % 8 == 0 else 1
    pt_flat = page_table.reshape(-1)

    def page_spec(u):
        return pl.BlockSpec((None, None, PAGE_ROWS, HEAD_W),
                            lambda b, p, pt: (l, pt[b * n_pages + p * pp + u], 0, 0))

    lam_spec = pl.BlockSpec((None, 1, ATT_HD), lambda b, p, pt: (l, 0, 0))
    new_spec = pl.BlockSpec((None, n_new_rows, HEAD_W), lambda b, p, pt: (b, 0, 0))
    grid_spec = pltpu.PrefetchScalarGridSpec(
        num_scalar_prefetch=1,
        grid=(nb, n_pages // pp),
        in_specs=([pl.BlockSpec((None, n_rows, HEAD_W), lambda b, p, pt: (b, 0, 0))]
                  + [page_spec(u) for u in range(pp)] * 2
                  + [new_spec, new_spec, lam_spec, lam_spec, lam_spec, lam_spec,
                     pl.BlockSpec((None, 1, HEAD_W), lambda b, p, pt: (l, 0, 0))]),
        out_specs=pl.BlockSpec((None, n_new_rows, HEAD_W), lambda b, p, pt: (b, 0, 0)),
        scratch_shapes=[pltpu.VMEM((n_rows, 1), F32), pltpu.VMEM((n_rows, 1), F32),
                        pltpu.VMEM((n_rows, HEAD_W), F32)],
    )
    return pl.pallas_call(
        functools.partial(_decode_kernel, pages_per_step=pp, lam_init=_lambda_init(l)),
        grid_spec=grid_spec,
        out_shape=jax.ShapeDtypeStruct((nb, n_new_rows, HEAD_W), F32),
        compiler_params=_cparams("parallel", "arbitrary"),
        name="decode_attn",
    )(pt_flat, q2, *([cache_k] * pp), *([cache_v] * pp), k_new, v_new, *lams, sg)


def _gla_kernel(*refs, chunk, rows, has_s0):
    if has_s0:
        (q_ref, k_ref, g_ref, v_ref, og_ref, ng_ref, ones_ref, s0_ref,
         o_ref, sout_ref, st_scr) = refs
    else:
        (q_ref, k_ref, g_ref, v_ref, og_ref, ng_ref, ones_ref,
         o_ref, sout_ref, st_scr) = refs
    t = pl.program_id(2)
    n_iter = q_ref.shape[0] // rows
    n_grp = rows // GLA_DIAG

    @pl.when(t == 0)
    def _():
        if has_s0:
            st_scr[...] = s0_ref[...].T
        else:
            st_scr[...] = jnp.zeros(st_scr.shape, F32)

    ones2 = ones_ref[...]
    rid = lax.broadcasted_iota(jnp.int32, (rows, HEAD_W), 0)
    in_chunk = rid % chunk
    in_diag = rid % GLA_DIAG
    blk_r = lax.broadcasted_iota(jnp.int32, (rows, rows), 0)
    blk_c = lax.broadcasted_iota(jnp.int32, (rows, rows), 1)

    def shift_in_block(x, j):
        return pltpu.roll(x.reshape(n_grp, GLA_DIAG, HEAD_W), j, 1).reshape(rows, HEAD_W)

    def one_block(it, _):
        r0 = pl.multiple_of(it * rows, rows)
        q = q_ref[pl.ds(r0, rows), :]
        k = k_ref[pl.ds(r0, rows), :]
        vb = v_ref[pl.ds(r0, rows), :]
        vf = vb.astype(F32)
        b = g_ref[pl.ds(r0, rows), :]
        s = 1
        while s < chunk:
            b = b + jnp.where(in_chunk >= s, pltpu.roll(b, s, 0), 0.0)
            s *= 2
        o = jnp.zeros((rows, HEAD_W), F32)
        for jp in range(GLA_DIAG // 2):
            parts, vs = [], []
            for j in (2 * jp, 2 * jp + 1):
                kj, bj, vj = (k, b, vf) if j == 0 else (shift_in_block(k, j), shift_in_block(b, j),
                                                        shift_in_block(vf, j))
                d = q * kj * jnp.exp(b - bj)
                parts.append(jnp.where(in_diag >= j, d, 0.0).astype(BF16))
                vs.append(vj)
            w = jnp.dot(jnp.concatenate(parts, axis=1), ones2, preferred_element_type=F32)
            o = o + w[:, :HEAD_W] * vs[0] + w[:, HEAD_W:] * vs[1]
        a = None
        half = GLA_DIAG
        while 2 * half <= chunk:
            blk = 2 * half
            b3 = b.reshape(rows // blk, blk, HEAD_W)
            ref = jnp.broadcast_to(b3[:, half - 1:half, :], b3.shape).reshape(rows, HEAD_W)
            second = rid % blk >= half
            qt = jnp.where(second, q * jnp.exp(b - ref), 0.0).astype(BF16)
            kt = jnp.where(second, 0.0, k * jnp.exp(ref - b)).astype(BF16)
            al = lax.dot_general(qt, kt, NT_DIMS, preferred_element_type=F32)
            al = jnp.where(blk_r // blk == blk_c // blk, al, 0.0)
            a = al if a is None else a + al
            half = blk
        if a is not None:
            o = o + jnp.dot(a.astype(BF16), vb, preferred_element_type=F32)
        st = st_scr[...]
        outs = []
        for c in range(rows // chunk):
            sl = slice(c * chunk, (c + 1) * chunk)
            bc = b[sl]
            b_end = bc[chunk - 1:chunk]
            qe = (q[sl] * jnp.exp(bc)).astype(BF16)
            outs.append(o[sl] + lax.dot_general(qe, st.astype(BF16), NT_DIMS,
                                                preferred_element_type=F32))
            kd = (k[sl] * jnp.exp(b_end - bc)).astype(BF16)
            st = st * jnp.exp(b_end) + lax.dot_general(vb[sl], kd, TN_DIMS,
                                                       preferred_element_type=F32)
        st_scr[...] = st
        o = outs[0] if len(outs) == 1 else jnp.concatenate(outs, axis=0)
        o_ref[pl.ds(r0, rows), :] = (_rms(o, ng_ref[...]) * og_ref[pl.ds(r0, rows), :]
                                     ).astype(o_ref.dtype)
        return 0

    lax.fori_loop(0, n_iter, one_block, 0)

    @pl.when(t == pl.num_programs(2) - 1)
    def _():
        sout_ref[...] = st_scr[...].T


def _gla(q, k, g, v, og, ng, s0, l, batch, seq, chunk):
    tb = _tile(seq, 1024)
    rows = _tile(tb, GLA_ROWS)
    nt_ = seq // tb
    ones2 = jnp.asarray(np.kron(np.eye(2, dtype=np.float32),
                                np.ones((HEAD_W, HEAD_W), np.float32)), dtype=BF16)
    row = pl.BlockSpec((tb, HEAD_W), lambda b, h, t: (b * nt_ + t, h))
    st_spec = pl.BlockSpec((None, None, HEAD_W, HEAD_W), lambda b, h, t: (b, h, 0, 0))
    in_specs = [row, row, row, row, row,
                pl.BlockSpec((None, 1, HEAD_W), lambda b, h, t: (l, 0, 0)),
                pl.BlockSpec(ones2.shape, lambda b, h, t: (0, 0))]
    args = [q, k, g, v, og, ng, ones2]
    if s0 is not None:
        in_specs.append(pl.BlockSpec((None, None, None, HEAD_W, HEAD_W),
                                     lambda b, h, t: (l, b, h, 0, 0)))
        args.append(s0)
    return pl.pallas_call(
        functools.partial(_gla_kernel, chunk=chunk, rows=rows, has_s0=s0 is not None),
        grid=(batch, N_HEADS, nt_),
        in_specs=in_specs,
        out_specs=[row, st_spec],
        out_shape=[jax.ShapeDtypeStruct((batch * seq, GROUP_W), BF16),
                   jax.ShapeDtypeStruct((batch, N_HEADS, HEAD_W, HEAD_W), F32)],
        scratch_shapes=[pltpu.VMEM((HEAD_W, HEAD_W), F32)],
        compiler_params=_cparams("parallel", "parallel", "arbitrary"),
        name="gla",
    )(*args)


def _merge_kernel(att_ref, o_ref, wa_ref, wh_ref, ga_ref, gb_ref, out_ref):
    a = jnp.dot(att_ref[...], wa_ref[...], preferred_element_type=F32)
    b = jnp.dot(o_ref[...], wh_ref[...], preferred_element_type=F32)
    out_ref[...] = (ga_ref[...].astype(F32) * a + gb_ref[...].astype(F32) * b).astype(out_ref.dtype)


def _merge(att, o, w_att, w_hg, gates, l):
    m = att.shape[0]
    d = w_att.shape[2]
    tm = _tile(m, 1024)
    tn = _tile(d, 1024)
    nj = d // tn
    return pl.pallas_call(
        _merge_kernel,
        grid=(m // tm, nj),
        in_specs=[pl.BlockSpec((tm, GROUP_W), lambda i, j: (i, 0)),
                  pl.BlockSpec((tm, GROUP_W), lambda i, j: (i, 0)),
                  pl.BlockSpec((None, GROUP_W, tn), lambda i, j: (l, 0, j)),
                  pl.BlockSpec((None, GROUP_W, tn), lambda i, j: (l, 0, j)),
                  pl.BlockSpec((tm, tn), lambda i, j: (i, j)),
                  pl.BlockSpec((tm, tn), lambda i, j: (i, nj + j))],
        out_specs=pl.BlockSpec((tm, tn), lambda i, j: (i, j)),
        out_shape=jax.ShapeDtypeStruct((m, d), BF16),
        compiler_params=_cparams("parallel", "arbitrary"),
        name="merge",
    )(att, o, w_att, w_hg, gates, gates)


def _out_proj_kernel(mix_ref, w_ref, x_ref, g_ref, o_ref):
    y = jnp.dot(mix_ref[...], w_ref[...], preferred_element_type=F32)
    o_ref[...] = x_ref[...] + _rms(y, g_ref[...])


def _out_proj(mixed, w_out, x, g, l):
    m, d = x.shape
    tm = _tile(m, 512)
    return pl.pallas_call(
        _out_proj_kernel,
        grid=(m // tm,),
        in_specs=[pl.BlockSpec((tm, d), lambda i: (i, 0)),
                  pl.BlockSpec((None, d, d), lambda i: (l, 0, 0)),
                  pl.BlockSpec((tm, d), lambda i: (i, 0)),
                  pl.BlockSpec((None, 1, d), lambda i: (l, 0, 0))],
        out_specs=pl.BlockSpec((tm, d), lambda i: (i, 0)),
        out_shape=jax.ShapeDtypeStruct((m, d), F32),
        compiler_params=_cparams("parallel"),
        name="out_proj",
    )(mixed, w_out, x, g)


def _ffn_kernel(x_ref, g1_ref, wu_ref, wd_ref, g2_ref, o_ref, h_scr):
    f = pl.program_id(1)

    @pl.when(f == 0)
    def _():
        h_scr[...] = _rms(x_ref[...], g1_ref[...]).astype(BF16)
        o_ref[...] = jnp.zeros(o_ref.shape, F32)

    u = jnp.dot(h_scr[...], wu_ref[...], preferred_element_type=F32)
    u = jnp.square(jnp.maximum(u, 0.0)).astype(BF16)
    o_ref[...] += jnp.dot(u, wd_ref[...], preferred_element_type=F32)

    @pl.when(f == pl.num_programs(1) - 1)
    def _():
        o_ref[...] = x_ref[...] + _rms(o_ref[...], g2_ref[...])


def _ffn(x, g1, w_up, w_down, g2, l):
    m, d = x.shape
    dff = w_up.shape[2]
    tm = _tile(m, 512)
    tf = _tile(dff, 1024)
    return pl.pallas_call(
        _ffn_kernel,
        grid=(m // tm, dff // tf),
        in_specs=[pl.BlockSpec((tm, d), lambda i, f: (i, 0)),
                  pl.BlockSpec((None, 1, d), lambda i, f: (l, 0, 0)),
                  pl.BlockSpec((None, d, tf), lambda i, f: (l, 0, f)),
                  pl.BlockSpec((None, tf, d), lambda i, f: (l, f, 0)),
                  pl.BlockSpec((None, 1, d), lambda i, f: (l, 0, 0))],
        out_specs=pl.BlockSpec((tm, d), lambda i, f: (i, 0)),
        out_shape=jax.ShapeDtypeStruct((m, d), F32),
        scratch_shapes=[pltpu.VMEM((tm, d), BF16)],
        compiler_params=_cparams("parallel", "arbitrary"),
        name="ffn",
    )(x, g1, w_up, w_down, g2)


def _rope_tables(pos):
    half = ROT_DIM // 2
    inv = ROPE_THETA ** (-jnp.arange(0, ROT_DIM, 2, dtype=F32) / ROT_DIM)
    ang = pos.astype(F32)[:, None] * inv[None, :]
    cos, sin = jnp.cos(ang), jnp.sin(ang)
    n = pos.shape[0]
    one = jnp.ones((n, ATT_HD - ROT_DIM), F32)
    zero = jnp.zeros((n, ATT_HD - ROT_DIM), F32)
    zh = jnp.zeros((n, half), F32)
    cos64 = jnp.concatenate([cos, cos, one], axis=1)
    sa64 = jnp.concatenate([-sin, zh, zero], axis=1)
    sb64 = jnp.concatenate([zh, sin, zero], axis=1)
    return tuple(jnp.concatenate([t, t], axis=1) for t in (cos64, sa64, sb64))


def _mix_inputs(x, wts, l, rope_tabs):
    h = _norm_cast(x, wts["g_pre_mix"], l)
    q, kf, kb, vf, vb = _proj_att(h, wts["w_in"], l, rope_tabs)
    hq, hk, hg, hv, og = _proj_hg(h, wts["w_in"], wts["hg_lower_bounds"], l)
    gates = _proj_gate(h, wts["w_in"], l)
    return q, kf, kb, vf, vb, hq, hk, hg, hv, og, gates


def _mix_outputs(x, att, o, gates, wts, l):
    mixed = _merge(att, o, wts["w_att_out"], wts["w_hg_out"], gates, l)
    x = _out_proj(mixed, wts["w_out"], x, wts["g_post_mix"], l)
    return _ffn(x, wts["g_pre_ffn"], wts["w_up"], wts["w_down"], wts["g_post_ffn"], l)


def kernel(x_prompt, x_sample, cache_k, cache_v, state_hgrn, page_table, g_pre_mix, w_in,
           lambda_q1, lambda_k1, lambda_q2, lambda_k2, subln_g, hg_lower_bounds, hg_norm_g,
           w_att_out, w_hg_out, w_out, g_post_mix, g_pre_ffn, w_up, w_down, g_post_ffn):
    depth = w_in.shape[0]
    b_p, t_p, d = x_prompt.shape
    b_s, t_s, _ = x_sample.shape
    n_pool = cache_k.shape[1]
    past_len = page_table.shape[1] * PAGE_SIZE

    def vec(a):
        return a.reshape(depth, 1, a.shape[-1])

    wts = {
        "g_pre_mix": vec(g_pre_mix), "g_post_mix": vec(g_post_mix),
        "g_pre_ffn": vec(g_pre_ffn), "g_post_ffn": vec(g_post_ffn),
        "hg_lower_bounds": hg_lower_bounds,
        "w_in": w_in.astype(BF16), "w_att_out": w_att_out.astype(BF16),
        "w_hg_out": w_hg_out.astype(BF16), "w_out": w_out.astype(BF16),
        "w_up": w_up.astype(BF16), "w_down": w_down.astype(BF16),
    }
    lams = [vec(a) for a in (lambda_q1, lambda_k1, lambda_q2, lambda_k2)]
    sg, ng = vec(subln_g), vec(hg_norm_g)
    ck = cache_k.reshape(depth, n_pool, PAGE_ROWS, HEAD_W)
    cv = cache_v.reshape(depth, n_pool, PAGE_ROWS, HEAD_W)

    tabs_p = _rope_tables(jnp.arange(t_p))
    tabs_s = _rope_tables(past_len + (jnp.arange(b_s * t_s) % t_s))
    chunk_p = GLA_CHUNK if t_p % GLA_CHUNK == 0 else t_p
    t_pad = 16
    lane_map = jnp.arange(HEAD_W) // ATT_HD

    xp = x_prompt.reshape(b_p * t_p, d)
    xs = x_sample.reshape(b_s * t_s, d)
    outs = [[] for _ in range(6)]
    for l in range(depth):
        q, kf, kb, vf, vb, hq, hk, hg, hv, og, gates = _mix_inputs(xp, wts, l, tabs_p)
        att = _flash(q, kb, vb, lams, sg, l, b_p, t_p)
        o, s_fin = _gla(hq, hk, hg, hv, og, ng, None, l, b_p, t_p, chunk_p)
        xp = _mix_outputs(xp, att, o, gates, wts, l)
        outs[0].append(kf.reshape(b_p, t_p, N_HEADS, HEAD_W))
        outs[1].append(vf.reshape(b_p, t_p, N_HEADS, HEAD_W))
        outs[2].append(s_fin)

        q, kf, kb, vf, vb, hq, hk, hg, hv, og, gates = _mix_inputs(xs, wts, l, tabs_s)
        qr = q.reshape(b_s, 1, t_s * N_HEADS, HEAD_W)
        q2 = jnp.where(lane_map[None, None, None, :] == jnp.arange(2)[None, :, None, None],
                       qr, jnp.zeros((), BF16)).reshape(b_s, 2 * t_s * N_HEADS, HEAD_W)
        att = _decode_attn(q2, ck, cv, page_table, kb.reshape(b_s, t_s * N_HEADS, HEAD_W),
                           vb.reshape(b_s, t_s * N_HEADS, HEAD_W), lams, sg, l)
        att = att.reshape(b_s * t_s, GROUP_W).astype(BF16)
        pad = lambda a: jnp.pad(a.reshape(b_s, t_s, GROUP_W),
                                ((0, 0), (0, t_pad - t_s), (0, 0))).reshape(b_s * t_pad, GROUP_W)
        o, s_fin = _gla(pad(hq), pad(hk), pad(hg), pad(hv), pad(og), ng, state_hgrn, l,
                        b_s, t_pad, t_pad)
        o = o.reshape(b_s, t_pad, GROUP_W)[:, :t_s].reshape(b_s * t_s, GROUP_W)
        xs = _mix_outputs(xs, att, o, gates, wts, l)
        outs[3].append(kf.reshape(b_s, t_s, N_HEADS, HEAD_W))
        outs[4].append(vf.reshape(b_s, t_s, N_HEADS, HEAD_W))
        outs[5].append(s_fin)

    return (xp.reshape(b_p, t_p, d), xs.reshape(b_s, t_s, d),
            jnp.stack(outs[0]), jnp.stack(outs[1]), jnp.stack(outs[2]),
            jnp.stack(outs[3]), jnp.stack(outs[4]), jnp.stack(outs[5]))
```

```python
import functools
import math

import numpy as np
import jax
import jax.numpy as jnp
from jax import lax
from jax.experimental import pallas as pl
from jax.experimental.pallas import tpu as pltpu

F32 = jnp.float32
BF16 = jnp.bfloat16

N_HEADS = 8
HEAD_W = 128
ATT_HD = 64
ROT_DIM = ATT_HD // 4
ROPE_THETA = 500000.0
PAGE_SIZE = 128
PAGE_ROWS = PAGE_SIZE * N_HEADS
GLA_CHUNK = 64
GLA_DIAG = 8
GLA_ROWS = 256
EPS = 1e-6
NEG_BIG = -1e30
LOG2_E = math.log2(math.e)
GROUP_W = N_HEADS * HEAD_W

VMEM_LIMIT_BYTES = 56 * 1024 * 1024

NT_DIMS = (((1,), (1,)), ((), ()))
TN_DIMS = (((0,), (0,)), ((), ()))


def _lambda_init(l):
    return 0.8 - 0.6 * math.exp(-0.3 * l)


def _tile(n, pref):
    return pref if n % pref == 0 else n


def _cparams(*sem):
    return pltpu.CompilerParams(dimension_semantics=sem, vmem_limit_bytes=VMEM_LIMIT_BYTES)


def _rms(x, g):
    return x * lax.rsqrt(jnp.mean(x * x, axis=-1, keepdims=True) + EPS) * g


def _norm_cast_kernel(x_ref, g_ref, o_ref):
    o_ref[...] = _rms(x_ref[...], g_ref[...]).astype(o_ref.dtype)


def _norm_cast(x, g, l):
    m, d = x.shape
    tm = _tile(m, 512)
    return pl.pallas_call(
        _norm_cast_kernel,
        grid=(m // tm,),
        in_specs=[pl.BlockSpec((tm, d), lambda i: (i, 0)),
                  pl.BlockSpec((None, 1, d), lambda i: (l, 0, 0))],
        out_specs=pl.BlockSpec((tm, d), lambda i: (i, 0)),
        out_shape=jax.ShapeDtypeStruct((m, d), BF16),
        compiler_params=_cparams("parallel"),
        name="norm_cast",
    )(x, g)


def _proj_kernel(h_ref, w_ref, *refs, epilogue, n_extra):
    extra, outs, wb_scr = refs[:n_extra], refs[n_extra:-1], refs[-1]

    @pl.when(pl.program_id(1) == 0)
    def _():
        wb_scr[...] = w_ref[...].astype(BF16)

    acc = jnp.dot(h_ref[...], wb_scr[...], preferred_element_type=F32)
    epilogue(acc, extra, outs)


def _proj(h, w_in, l, col0, tn, n_j, tm, epilogue, out_dtypes, name, extra=(), extra_specs=()):
    m, d = h.shape
    out_spec = pl.BlockSpec((tm, tn), lambda j, i: (i, j))
    return pl.pallas_call(
        functools.partial(_proj_kernel, epilogue=epilogue, n_extra=len(extra)),
        grid=(n_j, m // tm),
        in_specs=[pl.BlockSpec((tm, d), lambda j, i: (i, 0)),
                  pl.BlockSpec((None, d, tn), lambda j, i: (l, 0, col0 + j))] + list(extra_specs),
        out_specs=[out_spec] * len(out_dtypes),
        out_shape=[jax.ShapeDtypeStruct((m, n_j * tn), dt) for dt in out_dtypes],
        scratch_shapes=[pltpu.VMEM((d, tn), BF16)],
        compiler_params=_cparams("arbitrary", "arbitrary"),
        name=name,
    )(h, w_in, *extra)


def _rope(acc, tabs, c):
    cos_ref, sa_ref, sb_ref = tabs
    x = acc[:, c * HEAD_W:(c + 1) * HEAD_W]
    return (x * cos_ref[...] + pltpu.roll(x, HEAD_W - ROT_DIM // 2, 1) * sa_ref[...]
            + pltpu.roll(x, ROT_DIM // 2, 1) * sb_ref[...])


def _ep_q(acc, tabs, outs):
    for c in range(N_HEADS):
        outs[0][:, c * HEAD_W:(c + 1) * HEAD_W] = (
            _rope(acc, tabs, c) * (ATT_HD ** -0.5 * LOG2_E)).astype(BF16)


def _ep_k(acc, tabs, outs):
    for c in range(N_HEADS):
        y = _rope(acc, tabs, c)
        outs[0][:, c * HEAD_W:(c + 1) * HEAD_W] = y
        outs[1][:, c * HEAD_W:(c + 1) * HEAD_W] = y.astype(BF16)


def _ep_v(acc, extra, outs):
    outs[0][...] = acc
    outs[1][...] = acc.astype(BF16)


def _ep_hg_query(acc, extra, outs):
    outs[0][...] = acc * jax.nn.sigmoid(acc) * (HEAD_W ** -0.5)


def _ep_hg_forget(acc, extra, outs, *, layer):
    if layer == 0:
        gval = jax.nn.sigmoid(acc)
        logg = jnp.minimum(acc, 0.0) - jnp.log1p(jnp.exp(-jnp.abs(acc)))
    else:
        p = extra[0][...]
        e = jnp.exp(p - jnp.max(p, axis=0, keepdims=True))
        sm = e / jnp.sum(e, axis=0, keepdims=True)
        lb = jnp.sum(sm[:layer + 1], axis=0, keepdims=True) - sm[0:1]
        gval = lb + (1.0 - lb) * jax.nn.sigmoid(acc)
        logg = jnp.log(gval)
    outs[0][...] = 1.0 - gval
    outs[1][...] = logg


def _ep_cast(acc, extra, outs):
    outs[0][...] = acc.astype(BF16)


def _ep_silu(acc, extra, outs):
    outs[0][...] = acc * jax.nn.sigmoid(acc)


def _ep_sigmoid(acc, extra, outs):
    outs[0][...] = jax.nn.sigmoid(acc).astype(BF16)


def _project_all(h, w_in, lbp, l, rope_tabs):
    m, d = h.shape
    n_tab = rope_tabs[0].shape[0]
    tm_r = _tile(n_tab, 1024)
    tab = pl.BlockSpec((tm_r, HEAD_W), lambda j, i: (i % (n_tab // tm_r), 0))
    tm = _tile(m, 1024)
    grp = functools.partial(_proj, h, w_in, l, tn=GROUP_W, n_j=1)
    (q,) = grp(col0=0, tm=tm_r, epilogue=_ep_q, out_dtypes=[BF16], name="proj_q",
               extra=rope_tabs, extra_specs=[tab] * 3)
    kf, kb = grp(col0=1, tm=tm_r, epilogue=_ep_k, out_dtypes=[F32, BF16], name="proj_k",
                 extra=rope_tabs, extra_specs=[tab] * 3)
    vf, vb = grp(col0=2, tm=tm, epilogue=_ep_v, out_dtypes=[F32, BF16], name="proj_v")
    (hq,) = grp(col0=3, tm=tm, epilogue=_ep_hg_query, out_dtypes=[F32], name="proj_hq")
    hk, hg = grp(col0=4, tm=tm, epilogue=functools.partial(_ep_hg_forget, layer=l),
                 out_dtypes=[F32, F32], name="proj_hf", extra=(lbp,),
                 extra_specs=[pl.BlockSpec(lbp.shape, lambda j, i: (0, 0))])
    (hv,) = grp(col0=5, tm=tm, epilogue=_ep_cast, out_dtypes=[BF16], name="proj_hi")
    (og,) = grp(col0=6, tm=tm, epilogue=_ep_silu, out_dtypes=[F32], name="proj_hg")
    tn = _tile(2 * d, GROUP_W)
    (gates,) = _proj(h, w_in, l, col0=7 * GROUP_W // tn, tn=tn, n_j=2 * d // tn, tm=tm,
                     epilogue=_ep_sigmoid, out_dtypes=[BF16], name="proj_gate")
    return q, kf, kb, vf, vb, hq, hk, hg, hv, og, gates


def _lambda_value(lq1, lk1, lq2, lk2, lam_init):
    a = jnp.sum(lq1[...] * lk1[...], axis=1, keepdims=True)
    b = jnp.sum(lq2[...] * lk2[...], axis=1, keepdims=True)
    return jnp.exp(a) - jnp.exp(b) + lam_init


def _online_update(s, v, m, l, acc):
    m_new = jnp.maximum(m, jnp.max(s, axis=1, keepdims=True))
    alpha = jnp.exp2(m - m_new)
    p = jnp.exp2(s - m_new)
    l = alpha * l + jnp.sum(p, axis=1, keepdims=True)
    acc = alpha * acc + jnp.dot(p.astype(BF16), v, preferred_element_type=F32)
    return m_new, l, acc


def _flash_kernel(q_ref, k_ref, v_ref, lq1, lk1, lq2, lk2, sg_ref, o_ref, *, tq, lam_init):
    qi = pl.program_id(2)
    q = q_ref[...]
    lane = lax.broadcasted_iota(jnp.int32, q.shape, 1)
    q1 = jnp.where(lane < ATT_HD, q, jnp.zeros_like(q))
    q2 = jnp.where(lane >= ATT_HD, q, jnp.zeros_like(q))

    def update(s, vt, m, l, acc):
        m_new = jnp.maximum(m, jnp.max(s, axis=0, keepdims=True))
        alpha = jnp.exp2(m - m_new)
        p = jnp.exp2(s - m_new)
        l = alpha * l + jnp.sum(p, axis=0, keepdims=True)
        acc = alpha * acc + lax.dot_general(vt, p.astype(BF16), TN_DIMS,
                                            preferred_element_type=F32)
        return m_new, l, acc

    def step(carry, k0, n_keys, masked):
        m1, l1, a1, m2, l2, a2 = carry
        k0 = pl.multiple_of(k0, tq)
        kt = k_ref[pl.ds(k0, n_keys), :]
        vt = v_ref[pl.ds(k0, n_keys), :]
        s1 = lax.dot_general(kt, q1, NT_DIMS, preferred_element_type=F32)
        s2 = lax.dot_general(kt, q2, NT_DIMS, preferred_element_type=F32)
        if masked:
            key = k0 + lax.broadcasted_iota(jnp.int32, s1.shape, 0)
            qry = qi * tq + lax.broadcasted_iota(jnp.int32, s1.shape, 1)
            s1 = jnp.where(key <= qry, s1, NEG_BIG)
            s2 = jnp.where(key <= qry, s2, NEG_BIG)
        m1, l1, a1 = update(s1, vt, m1, l1, a1)
        m2, l2, a2 = update(s2, vt, m2, l2, a2)
        return m1, l1, a1, m2, l2, a2

    m0 = jnp.full((1, tq), NEG_BIG, F32)
    l0 = jnp.zeros((1, tq), F32)
    a0 = jnp.zeros((HEAD_W, tq), F32)
    n_wide = qi // 2
    carry = lax.fori_loop(0, n_wide, lambda j, c: step(c, j * (2 * tq), 2 * tq, False),
                          (m0, l0, a0, m0, l0, a0))
    m1, l1, a1, m2, l2, a2 = lax.cond(
        qi % 2 == 1,
        lambda c: step(c, n_wide * (2 * tq), 2 * tq, True),
        lambda c: step(c, qi * tq, tq, True),
        carry)
    lam = _lambda_value(lq1, lk1, lq2, lk2, lam_init)
    att = (a1 / l1 - lam * (a2 / l2)).T
    o_ref[...] = (_rms(att, sg_ref[...]) * (1.0 - lam_init)).astype(o_ref.dtype)


def _flash(q, k, v, lams, sg, l, batch, seq):
    tq = _tile(seq, 512)
    nq = seq // tq
    lam_spec = pl.BlockSpec((None, 1, ATT_HD), lambda b, h, i: (l, 0, 0))
    return pl.pallas_call(
        functools.partial(_flash_kernel, tq=tq, lam_init=_lambda_init(l)),
        grid=(batch, N_HEADS, nq),
        in_specs=[pl.BlockSpec((tq, HEAD_W), lambda b, h, i: (b * nq + i, h)),
                  pl.BlockSpec((seq, HEAD_W), lambda b, h, i: (b, h)),
                  pl.BlockSpec((seq, HEAD_W), lambda b, h, i: (b, h)),
                  lam_spec, lam_spec, lam_spec, lam_spec,
                  pl.BlockSpec((None, 1, HEAD_W), lambda b, h, i: (l, 0, 0))],
        out_specs=pl.BlockSpec((tq, HEAD_W), lambda b, h, i: (b * nq + i, h)),
        out_shape=jax.ShapeDtypeStruct((batch * seq, GROUP_W), BF16),
        compiler_params=_cparams("parallel", "parallel", "arbitrary"),
        name="flash_prompt",
    )(q, k, v, *lams, sg)


def _decode_kernel(pt_ref, q_ref, *refs, pages_per_step, lam_init):
    del pt_ref
    pp = pages_per_step
    k_refs, v_refs = refs[:pp], refs[pp:2 * pp]
    (kn_ref, vn_ref, lq1, lk1, lq2, lk2, sg_ref, o_ref, m_scr, l_scr, acc_scr) = refs[2 * pp:]
    p = pl.program_id(1)
    n_rows = q_ref.shape[0]
    half = n_rows // 2

    @pl.when(p == 0)
    def _():
        m_scr[...] = jnp.full(m_scr.shape, NEG_BIG, F32)
        l_scr[...] = jnp.zeros(l_scr.shape, F32)
        acc_scr[...] = jnp.zeros(acc_scr.shape, F32)

    q = q_ref[...]

    def own_head(n_cols):
        r = lax.broadcasted_iota(jnp.int32, (n_rows, n_cols), 0)
        c = lax.broadcasted_iota(jnp.int32, (n_rows, n_cols), 1)
        return r % N_HEADS == c % N_HEADS, r, c

    valid, _, _ = own_head(PAGE_ROWS)
    scores = []
    m_prev = m_scr[...]
    m_new = m_prev
    for u in range(pp):
        s = lax.dot_general(q, k_refs[u][...].astype(BF16), NT_DIMS, preferred_element_type=F32)
        s = jnp.where(valid, s, NEG_BIG)
        scores.append(s)
        m_new = jnp.maximum(m_new, jnp.max(s, axis=1, keepdims=True))
    alpha = jnp.exp2(m_prev - m_new)
    lsum = alpha * l_scr[...]
    acc = alpha * acc_scr[...]
    for u in range(pp):
        pu = jnp.exp2(scores[u] - m_new)
        lsum = lsum + jnp.sum(pu, axis=1, keepdims=True)
        acc = acc + jnp.dot(pu.astype(BF16), v_refs[u][...].astype(BF16),
                            preferred_element_type=F32)
    m_scr[...], l_scr[...], acc_scr[...] = m_new, lsum, acc

    @pl.when(p == pl.num_programs(1) - 1)
    def _():
        n_new_rows = kn_ref.shape[0]
        s = lax.dot_general(q, kn_ref[...], NT_DIMS, preferred_element_type=F32)
        same_head, r, c = own_head(n_new_rows)
        causal = c // N_HEADS <= (r % half) // N_HEADS
        s = jnp.where(same_head & causal, s, NEG_BIG)
        _, l2, acc2 = _online_update(s, vn_ref[...], m_new, lsum, acc)
        out = acc2 / l2
        lam = _lambda_value(lq1, lk1, lq2, lk2, lam_init)
        att = out[:half] - lam * out[half:]
        o_ref[...] = _rms(att, sg_ref[...]) * (1.0 - lam_init)


def _decode_attn(q2, cache_k, cache_v, page_table, k_new, v_new, lams, sg, l):
    nb, n_rows, _ = q2.shape
    n_new_rows = k_new.shape[1]
    n_pages = page_table.shape[1]
    pp = 8 if n_pages % 8 == 0 else 1
    pt_flat = page_table.reshape(-1)

    def page_spec(u):
        return pl.BlockSpec((None, None, PAGE_ROWS, HEAD_W),
                            lambda b, p, pt: (l, pt[b * n_pages + p * pp + u], 0, 0))

    lam_spec = pl.BlockSpec((None, 1, ATT_HD), lambda b, p, pt: (l, 0, 0))
    new_spec = pl.BlockSpec((None, n_new_rows, HEAD_W), lambda b, p, pt: (b, 0, 0))
    grid_spec = pltpu.PrefetchScalarGridSpec(
        num_scalar_prefetch=1,
        grid=(nb, n_pages // pp),
        in_specs=([pl.BlockSpec((None, n_rows, HEAD_W), lambda b, p, pt: (b, 0, 0))]
                  + [page_spec(u) for u in range(pp)] * 2
                  + [new_spec, new_spec, lam_spec, lam_spec, lam_spec, lam_spec,
                     pl.BlockSpec((None, 1, HEAD_W), lambda b, p, pt: (l, 0, 0))]),
        out_specs=pl.BlockSpec((None, n_new_rows, HEAD_W), lambda b, p, pt: (b, 0, 0)),
        scratch_shapes=[pltpu.VMEM((n_rows, 1), F32), pltpu.VMEM((n_rows, 1), F32),
                        pltpu.VMEM((n_rows, HEAD_W), F32)],
    )
    return pl.pallas_call(
        functools.partial(_decode_kernel, pages_per_step=pp, lam_init=_lambda_init(l)),
        grid_spec=grid_spec,
        out_shape=jax.ShapeDtypeStruct((nb, n_new_rows, HEAD_W), F32),
        compiler_params=_cparams("parallel", "arbitrary"),
        name="decode_attn",
    )(pt_flat, q2, *([cache_k] * pp), *([cache_v] * pp), k_new, v_new, *lams, sg)


def _gla_kernel(*refs, chunk, rows, has_s0):
    if has_s0:
        (q_ref, k_ref, g_ref, v_ref, og_ref, ng_ref, ones_ref, s0_ref,
         o_ref, sout_ref, st_scr) = refs
    else:
        (q_ref, k_ref, g_ref, v_ref, og_ref, ng_ref, ones_ref,
         o_ref, sout_ref, st_scr) = refs
    t = pl.program_id(2)
    n_iter = q_ref.shape[0] // rows
    n_grp = rows // GLA_DIAG

    @pl.when(t == 0)
    def _():
        if has_s0:
            st_scr[...] = s0_ref[...].T
        else:
            st_scr[...] = jnp.zeros(st_scr.shape, F32)

    ones2 = ones_ref[...]
    rid = lax.broadcasted_iota(jnp.int32, (rows, HEAD_W), 0)
    in_chunk = rid % chunk
    in_diag = rid % GLA_DIAG
    blk_r = lax.broadcasted_iota(jnp.int32, (rows, rows), 0)
    blk_c = lax.broadcasted_iota(jnp.int32, (rows, rows), 1)

    def shift_in_block(x, j):
        return pltpu.roll(x.reshape(n_grp, GLA_DIAG, HEAD_W), j, 1).reshape(rows, HEAD_W)

    def one_block(it, _):
        r0 = pl.multiple_of(it * rows, rows)
        q = q_ref[pl.ds(r0, rows), :]
        k = k_ref[pl.ds(r0, rows), :]
        vb = v_ref[pl.ds(r0, rows), :]
        vf = vb.astype(F32)
        b = g_ref[pl.ds(r0, rows), :]
        s = 1
        while s < chunk:
            b = b + jnp.where(in_chunk >= s, pltpu.roll(b, s, 0), 0.0)
            s *= 2
        o = jnp.zeros((rows, HEAD_W), F32)
        for jp in range(GLA_DIAG // 2):
            parts, vs = [], []
            for j in (2 * jp, 2 * jp + 1):
                kj, bj, vj = (k, b, vf) if j == 0 else (shift_in_block(k, j), shift_in_block(b, j),
                                                        shift_in_block(vf, j))
                d = q * kj * jnp.exp(b - bj)
                parts.append(jnp.where(in_diag >= j, d, 0.0).astype(BF16))
                vs.append(vj)
            w = jnp.dot(jnp.concatenate(parts, axis=1), ones2, preferred_element_type=F32)
            o = o + w[:, :HEAD_W] * vs[0] + w[:, HEAD_W:] * vs[1]
        a = None
        half = GLA_DIAG
        while 2 * half <= chunk:
            blk = 2 * half
            b3 = b.reshape(rows // blk, blk, HEAD_W)
            ref = jnp.broadcast_to(b3[:, half - 1:half, :], b3.shape).reshape(rows, HEAD_W)
            second = rid % blk >= half
            qt = jnp.where(second, q * jnp.exp(b - ref), 0.0).astype(BF16)
            kt = jnp.where(second, 0.0, k * jnp.exp(ref - b)).astype(BF16)
            al = lax.dot_general(qt, kt, NT_DIMS, preferred_element_type=F32)
            al = jnp.where(blk_r // blk == blk_c // blk, al, 0.0)
            a = al if a is None else a + al
            half = blk
        if a is not None:
            o = o + jnp.dot(a.astype(BF16), vb, preferred_element_type=F32)
        st = st_scr[...]
        outs = []
        for c in range(rows // chunk):
            sl = slice(c * chunk, (c + 1) * chunk)
            bc = b[sl]
            b_end = bc[chunk - 1:chunk]
            qe = (q[sl] * jnp.exp(bc)).astype(BF16)
            outs.append(o[sl] + lax.dot_general(qe, st.astype(BF16), NT_DIMS,
                                                preferred_element_type=F32))
            kd = (k[sl] * jnp.exp(b_end - bc)).astype(BF16)
            st = st * jnp.exp(b_end) + lax.dot_general(vb[sl], kd, TN_DIMS,
                                                       preferred_element_type=F32)
        st_scr[...] = st
        o = outs[0] if len(outs) == 1 else jnp.concatenate(outs, axis=0)
        o_ref[pl.ds(r0, rows), :] = (_rms(o, ng_ref[...]) * og_ref[pl.ds(r0, rows), :]
                                     ).astype(o_ref.dtype)
        return 0

    lax.fori_loop(0, n_iter, one_block, 0)

    @pl.when(t == pl.num_programs(2) - 1)
    def _():
        sout_ref[...] = st_scr[...].T


def _gla(q, k, g, v, og, ng, s0, l, batch, seq, chunk):
    tb = _tile(seq, 1024)
    rows = _tile(tb, GLA_ROWS)
    nt_ = seq // tb
    ones2 = jnp.asarray(np.kron(np.eye(2, dtype=np.float32),
                                np.ones((HEAD_W, HEAD_W), np.float32)), dtype=BF16)
    row = pl.BlockSpec((tb, HEAD_W), lambda b, h, t: (b * nt_ + t, h))
    st_spec = pl.BlockSpec((None, None, HEAD_W, HEAD_W), lambda b, h, t: (b, h, 0, 0))
    in_specs = [row, row, row, row, row,
                pl.BlockSpec((None, 1, HEAD_W), lambda b, h, t: (l, 0, 0)),
                pl.BlockSpec(ones2.shape, lambda b, h, t: (0, 0))]
    args = [q, k, g, v, og, ng, ones2]
    if s0 is not None:
        in_specs.append(pl.BlockSpec((None, None, None, HEAD_W, HEAD_W),
                                     lambda b, h, t: (l, b, h, 0, 0)))
        args.append(s0)
    return pl.pallas_call(
        functools.partial(_gla_kernel, chunk=chunk, rows=rows, has_s0=s0 is not None),
        grid=(batch, N_HEADS, nt_),
        in_specs=in_specs,
        out_specs=[row, st_spec],
        out_shape=[jax.ShapeDtypeStruct((batch * seq, GROUP_W), BF16),
                   jax.ShapeDtypeStruct((batch, N_HEADS, HEAD_W, HEAD_W), F32)],
        scratch_shapes=[pltpu.VMEM((HEAD_W, HEAD_W), F32)],
        compiler_params=_cparams("parallel", "parallel", "arbitrary"),
        name="gla",
    )(*args)


def _merge_kernel(att_ref, o_ref, wa_ref, wh_ref, ga_ref, gb_ref, out_ref):
    a = jnp.dot(att_ref[...], wa_ref[...], preferred_element_type=F32)
    b = jnp.dot(o_ref[...], wh_ref[...], preferred_element_type=F32)
    out_ref[...] = (ga_ref[...].astype(F32) * a + gb_ref[...].astype(F32) * b).astype(out_ref.dtype)


def _merge(att, o, w_att, w_hg, gates, l):
    m = att.shape[0]
    d = w_att.shape[2]
    tm = _tile(m, 1024)
    tn = _tile(d, 1024)
    nj = d // tn
    return pl.pallas_call(
        _merge_kernel,
        grid=(m // tm, nj),
        in_specs=[pl.BlockSpec((tm, GROUP_W), lambda i, j: (i, 0)),
                  pl.BlockSpec((tm, GROUP_W), lambda i, j: (i, 0)),
                  pl.BlockSpec((None, GROUP_W, tn), lambda i, j: (l, 0, j)),
                  pl.BlockSpec((None, GROUP_W, tn), lambda i, j: (l, 0, j)),
                  pl.BlockSpec((tm, tn), lambda i, j: (i, j)),
                  pl.BlockSpec((tm, tn), lambda i, j: (i, nj + j))],
        out_specs=pl.BlockSpec((tm, tn), lambda i, j: (i, j)),
        out_shape=jax.ShapeDtypeStruct((m, d), BF16),
        compiler_params=_cparams("parallel", "arbitrary"),
        name="merge",
    )(att, o, w_att, w_hg, gates, gates)


def _out_proj_kernel(mix_ref, w_ref, x_ref, g_ref, o_ref):
    y = jnp.dot(mix_ref[...], w_ref[...], preferred_element_type=F32)
    o_ref[...] = x_ref[...] + _rms(y, g_ref[...])


def _out_proj(mixed, w_out, x, g, l):
    m, d = x.shape
    tm = _tile(m, 512)
    return pl.pallas_call(
        _out_proj_kernel,
        grid=(m // tm,),
        in_specs=[pl.BlockSpec((tm, d), lambda i: (i, 0)),
                  pl.BlockSpec((None, d, d), lambda i: (l, 0, 0)),
                  pl.BlockSpec((tm, d), lambda i: (i, 0)),
                  pl.BlockSpec((None, 1, d), lambda i: (l, 0, 0))],
        out_specs=pl.BlockSpec((tm, d), lambda i: (i, 0)),
        out_shape=jax.ShapeDtypeStruct((m, d), F32),
        compiler_params=_cparams("parallel"),
        name="out_proj",
    )(mixed, w_out, x, g)


def _ffn_kernel(x_ref, g1_ref, wu_ref, wd_ref, g2_ref, o_ref, h_scr):
    f = pl.program_id(1)

    @pl.when(f == 0)
    def _():
        h_scr[...] = _rms(x_ref[...], g1_ref[...]).astype(BF16)
        o_ref[...] = jnp.zeros(o_ref.shape, F32)

    u = jnp.dot(h_scr[...], wu_ref[...], preferred_element_type=F32)
    u = jnp.square(jnp.maximum(u, 0.0)).astype(BF16)
    o_ref[...] += jnp.dot(u, wd_ref[...], preferred_element_type=F32)

    @pl.when(f == pl.num_programs(1) - 1)
    def _():
        o_ref[...] = x_ref[...] + _rms(o_ref[...], g2_ref[...])


def _ffn(x, g1, w_up, w_down, g2, l):
    m, d = x.shape
    dff = w_up.shape[2]
    tm = _tile(m, 512)
    tf = _tile(dff, 1024)
    return pl.pallas_call(
        _ffn_kernel,
        grid=(m // tm, dff // tf),
        in_specs=[pl.BlockSpec((tm, d), lambda i, f: (i, 0)),
                  pl.BlockSpec((None, 1, d), lambda i, f: (l, 0, 0)),
                  pl.BlockSpec((None, d, tf), lambda i, f: (l, 0, f)),
                  pl.BlockSpec((None, tf, d), lambda i, f: (l, f, 0)),
                  pl.BlockSpec((None, 1, d), lambda i, f: (l, 0, 0))],
        out_specs=pl.BlockSpec((tm, d), lambda i, f: (i, 0)),
        out_shape=jax.ShapeDtypeStruct((m, d), F32),
        scratch_shapes=[pltpu.VMEM((tm, d), BF16)],
        compiler_params=_cparams("parallel", "arbitrary"),
        name="ffn",
    )(x, g1, w_up, w_down, g2)


def _rope_tables(pos):
    half = ROT_DIM // 2
    inv = ROPE_THETA ** (-jnp.arange(0, ROT_DIM, 2, dtype=F32) / ROT_DIM)
    ang = pos.astype(F32)[:, None] * inv[None, :]
    cos, sin = jnp.cos(ang), jnp.sin(ang)
    n = pos.shape[0]
    one = jnp.ones((n, ATT_HD - ROT_DIM), F32)
    zero = jnp.zeros((n, ATT_HD - ROT_DIM), F32)
    zh = jnp.zeros((n, half), F32)
    cos64 = jnp.concatenate([cos, cos, one], axis=1)
    sa64 = jnp.concatenate([-sin, zh, zero], axis=1)
    sb64 = jnp.concatenate([zh, sin, zero], axis=1)
    return tuple(jnp.concatenate([t, t], axis=1) for t in (cos64, sa64, sb64))


def _mix_inputs(x, wts, l, rope_tabs):
    h = _norm_cast(x, wts["g_pre_mix"], l)
    return _project_all(h, wts["w_in"], wts["hg_lower_bounds"], l, rope_tabs)


def _mix_outputs(x, att, o, gates, wts, l):
    mixed = _merge(att, o, wts["w_att_out"], wts["w_hg_out"], gates, l)
    x = _out_proj(mixed, wts["w_out"], x, wts["g_post_mix"], l)
    return _ffn(x, wts["g_pre_ffn"], wts["w_up"], wts["w_down"], wts["g_post_ffn"], l)


def kernel(x_prompt, x_sample, cache_k, cache_v, state_hgrn, page_table, g_pre_mix, w_in,
           lambda_q1, lambda_k1, lambda_q2, lambda_k2, subln_g, hg_lower_bounds, hg_norm_g,
           w_att_out, w_hg_out, w_out, g_post_mix, g_pre_ffn, w_up, w_down, g_post_ffn):
    depth = w_in.shape[0]
    b_p, t_p, d = x_prompt.shape
    b_s, t_s, _ = x_sample.shape
    n_pool = cache_k.shape[1]
    past_len = page_table.shape[1] * PAGE_SIZE

    def vec(a):
        return a.reshape(depth, 1, a.shape[-1])

    wts = {
        "g_pre_mix": vec(g_pre_mix), "g_post_mix": vec(g_post_mix),
        "g_pre_ffn": vec(g_pre_ffn), "g_post_ffn": vec(g_post_ffn),
        "hg_lower_bounds": hg_lower_bounds,
        "w_in": w_in, "w_att_out": w_att_out.astype(BF16),
        "w_hg_out": w_hg_out.astype(BF16), "w_out": w_out.astype(BF16),
        "w_up": w_up.astype(BF16), "w_down": w_down.astype(BF16),
    }
    lams = [vec(a) for a in (lambda_q1, lambda_k1, lambda_q2, lambda_k2)]
    sg, ng = vec(subln_g), vec(hg_norm_g)
    ck = cache_k.reshape(depth, n_pool, PAGE_ROWS, HEAD_W)
    cv = cache_v.reshape(depth, n_pool, PAGE_ROWS, HEAD_W)

    tabs_p = _rope_tables(jnp.arange(t_p))
    tabs_s = _rope_tables(past_len + (jnp.arange(b_s * t_s) % t_s))
    chunk_p = GLA_CHUNK if t_p % GLA_CHUNK == 0 else t_p
    t_pad = 16
    lane_map = jnp.arange(HEAD_W) // ATT_HD

    xp = x_prompt.reshape(b_p * t_p, d)
    xs = x_sample.reshape(b_s * t_s, d)
    outs = [[] for _ in range(6)]
    for l in range(depth):
        q, kf, kb, vf, vb, hq, hk, hg, hv, og, gates = _mix_inputs(xp, wts, l, tabs_p)
        att = _flash(q, kb, vb, lams, sg, l, b_p, t_p)
        o, s_fin = _gla(hq, hk, hg, hv, og, ng, None, l, b_p, t_p, chunk_p)
        xp = _mix_outputs(xp, att, o, gates, wts, l)
        outs[0].append(kf.reshape(b_p, t_p, N_HEADS, HEAD_W))
        outs[1].append(vf.reshape(b_p, t_p, N_HEADS, HEAD_W))
        outs[2].append(s_fin)

        q, kf, kb, vf, vb, hq, hk, hg, hv, og, gates = _mix_inputs(xs, wts, l, tabs_s)
        qr = q.reshape(b_s, 1, t_s * N_HEADS, HEAD_W)
        q2 = jnp.where(lane_map[None, None, None, :] == jnp.arange(2)[None, :, None, None],
                       qr, jnp.zeros((), BF16)).reshape(b_s, 2 * t_s * N_HEADS, HEAD_W)
        att = _decode_attn(q2, ck, cv, page_table, kb.reshape(b_s, t_s * N_HEADS, HEAD_W),
                           vb.reshape(b_s, t_s * N_HEADS, HEAD_W), lams, sg, l)
        att = att.reshape(b_s * t_s, GROUP_W).astype(BF16)
        pad = lambda a: jnp.pad(a.reshape(b_s, t_s, GROUP_W),
                                ((0, 0), (0, t_pad - t_s), (0, 0))).reshape(b_s * t_pad, GROUP_W)
        o, s_fin = _gla(pad(hq), pad(hk), pad(hg), pad(hv), pad(og), ng, state_hgrn, l,
                        b_s, t_pad, t_pad)
        o = o.reshape(b_s, t_pad, GROUP_W)[:, :t_s].reshape(b_s * t_s, GROUP_W)
        xs = _mix_outputs(xs, att, o, gates, wts, l)
        outs[3].append(kf.reshape(b_s, t_s, N_HEADS, HEAD_W))
        outs[4].append(vf.reshape(b_s, t_s, N_HEADS, HEAD_W))
        outs[5].append(s_fin)

    return (xp.reshape(b_p, t_p, d), xs.reshape(b_s, t_s, d),
            jnp.stack(outs[0]), jnp.stack(outs[1]), jnp.stack(outs[2]),
            jnp.stack(outs[3]), jnp.stack(outs[4]), jnp.stack(outs[5]))
```
